```python
import math
import jax, jax.numpy as jnp
from jax import lax
import numpy as np

D_MODEL = 2048
BATCH = 4
SEQ = 2048
DEPTH = 1

N_META = 16
CHUNK = 128
PAD = CHUNK - N_META
D_SSM = D_MODEL
D_ATT = D_MODEL
D_MIX = D_SSM + D_ATT
SSM_HEAD_DIM = 64
SSM_HEADS = D_SSM // SSM_HEAD_DIM
SSM_GROUPS = 8
SSM_HPG = SSM_HEADS // SSM_GROUPS
D_STATE = 128
CONV_K = 4
D_CONV = D_SSM + 2 * SSM_GROUPS * D_STATE
ATT_V_DIM = 128
ATT_HEADS = D_ATT // ATT_V_DIM
ATT_QK_DIM = ATT_V_DIM // 2
D_FF = 4 * D_MODEL
N_IN = D_SSM + D_CONV + SSM_HEADS + 3 * D_ATT
ALPHA = (2 * DEPTH) ** 0.25
BETA = (8 * DEPTH) ** -0.25
LN_EPS = 1e-5
RMS_EPS = 1e-5
NEG_INF = -1e30

kernel_name = "hymba_ssd_diffattn_alibi_deepnorm"


def layer_norm(x, g, b):
    xf = x.astype(jnp.float32)
    mu = xf.mean(-1, keepdims=True)
    var = jnp.square(xf - mu).mean(-1, keepdims=True)
    return ((xf - mu) * lax.rsqrt(var + LN_EPS) * g.astype(jnp.float32) + b.astype(jnp.float32)).astype(x.dtype)


def rms_norm(x, w):
    xf = x.astype(jnp.float32)
    xf = xf * lax.rsqrt(jnp.mean(xf * xf, -1, keepdims=True) + RMS_EPS)
    return xf * w.astype(jnp.float32)


def gated_group_rms_norm(y, z, w):
    g = y.astype(jnp.float32) * jax.nn.silu(z.astype(jnp.float32))
    shp = g.shape
    g = g.reshape(*shp[:-1], SSM_GROUPS, shp[-1] // SSM_GROUPS)
    g = g * lax.rsqrt(jnp.mean(g * g, -1, keepdims=True) + RMS_EPS)
    return g.reshape(shp) * w.astype(jnp.float32)


def segsum(a):
    cs = jnp.cumsum(a, axis=-1)
    T = a.shape[-1]
    diff = cs[..., :, None] - cs[..., None, :]
    mask = jnp.tril(jnp.ones((T, T), dtype=bool))
    return jnp.where(mask, diff, -jnp.inf)


def causal_dwconv(u, w, b):
    out = lax.conv_general_dilated(
        u, w[:, None, :].astype(u.dtype), window_strides=(1,), padding=[(CONV_K - 1, 0)],
        dimension_numbers=("NWC", "WIO", "NWC"), feature_group_count=u.shape[-1])
    return out + b.astype(u.dtype)


def ssd_chunk_scan(X, a, Bm, Cm):
    b, Lp = X.shape[:2]
    nc = Lp // CHUNK
    f32 = jnp.float32
    X = X.astype(f32).reshape(b, nc, CHUNK, SSM_GROUPS, SSM_HPG, SSM_HEAD_DIM)
    Bm = Bm.astype(f32).reshape(b, nc, CHUNK, SSM_GROUPS, D_STATE)
    Cm = Cm.astype(f32).reshape(b, nc, CHUNK, SSM_GROUPS, D_STATE)
    a = a.astype(f32).reshape(b, nc, CHUNK, SSM_GROUPS, SSM_HPG).transpose(0, 3, 4, 1, 2)
    acs = jnp.cumsum(a, axis=-1)
    Lmat = jnp.exp(segsum(a))
    y_diag = jnp.einsum("bclgn,bcsgn,bgrcls,bcsgrp->bclgrp", Cm, Bm, Lmat, X)
    decay_states = jnp.exp(acs[..., -1:] - acs)
    states = jnp.einsum("bclgn,bgrcl,bclgrp->bcgrpn", Bm, decay_states, X)
    states = jnp.concatenate([jnp.zeros_like(states[:, :1]), states], axis=1)
    chunk_a = jnp.pad(acs[..., -1], ((0, 0), (0, 0), (0, 0), (1, 0)))
    chunk_decay = jnp.exp(segsum(chunk_a))
    states = jnp.einsum("bgrzc,bcgrpn->bzgrpn", chunk_decay, states)[:, :-1]
    y_off = jnp.einsum("bclgn,bcgrpn,bgrcl->bclgrp", Cm, states, jnp.exp(acs))
    return (y_diag + y_off).reshape(b, Lp, SSM_HEADS, SSM_HEAD_DIM)


def alibi_slopes(n):
    return jnp.asarray(2.0 ** (-8.0 * np.arange(1, n + 1) / n), dtype=jnp.float32)


def hybrid_mixer(h, w_in, conv_w, conv_b, dt_bias, a_log, d_skip, ssd_norm_w,
                 lam_q1, lam_k1, lam_q2, lam_k2, attn_norm_w, w_out, lambda_init):
    b, L, _ = h.shape
    f32 = jnp.float32
    hp = jnp.pad(h, ((0, 0), (PAD, 0), (0, 0)))
    Lp = L + PAD
    proj = hp @ w_in
    o1 = D_SSM
    o2 = o1 + D_CONV
    o3 = o2 + SSM_HEADS
    o4 = o3 + D_ATT
    o5 = o4 + D_ATT
    z, xbc, dt_raw, q, k, v = jnp.split(proj, [o1, o2, o3, o4, o5], axis=-1)
    valid = jnp.arange(Lp) >= PAD

    xbc = jax.nn.silu(causal_dwconv(xbc, conv_w, conv_b))
    xs, Bm, Cm = jnp.split(xbc, [D_SSM, D_SSM + SSM_GROUPS * D_STATE], axis=-1)
    dt = jax.nn.softplus(dt_raw.astype(f32) + dt_bias.astype(f32))
    dt = jnp.where(valid[None, :, None], dt, 0.0)
    A = -jnp.exp(a_log.astype(f32))
    xs = xs.reshape(b, Lp, SSM_HEADS, SSM_HEAD_DIM)
    y = ssd_chunk_scan(xs * dt[..., None], dt * A,
                       Bm.reshape(b, Lp, SSM_GROUPS, D_STATE), Cm.reshape(b, Lp, SSM_GROUPS, D_STATE))
    y = y + xs.astype(f32) * d_skip.astype(f32)[:, None]
    y = gated_group_rms_norm(y.reshape(b, Lp, D_SSM), z, ssd_norm_w).astype(h.dtype)

    q = q.reshape(b, Lp, ATT_HEADS, 2, ATT_QK_DIM)
    k = k.reshape(b, Lp, ATT_HEADS, 2, ATT_QK_DIM)
    v = v.reshape(b, Lp, ATT_HEADS, ATT_V_DIM)
    lam = (jnp.exp(jnp.sum(lam_q1.astype(f32) * lam_k1.astype(f32)))
           - jnp.exp(jnp.sum(lam_q2.astype(f32) * lam_k2.astype(f32))) + lambda_init)
    slopes = alibi_slopes(ATT_HEADS)
    scale = ATT_QK_DIM ** -0.5
    pos = jnp.arange(Lp)
    outs = []
    for i in range(Lp // CHUNK):
        q0, kend = i * CHUNK, (i + 1) * CHUNK
        s = jnp.einsum("bqhcd,bkhcd->bhcqk", q[:, q0:kend], k[:, :kend],
                       preferred_element_type=f32) * scale
        dist = (pos[q0:kend, None] - pos[None, :kend]).astype(f32)
        mask = (dist >= 0) & valid[None, :kend]
        s = s - slopes[None, :, None, None, None] * dist[None, None, None]
        s = jnp.where(mask[None, None, None], s, NEG_INF)
        p = jax.nn.softmax(s, axis=-1)
        att = p[:, :, 0] - lam * p[:, :, 1]
        outs.append(jnp.einsum("bhqk,bkhd->bqhd", att.astype(v.dtype), v[:, :kend]))
    o = jnp.concatenate(outs, axis=1)
    o = (rms_norm(o, attn_norm_w) * (1.0 - lambda_init)).reshape(b, Lp, D_ATT).astype(h.dtype)

    mixed = jnp.concatenate([y, o], axis=-1)[:, PAD:]
    return mixed @ w_out


def setup_inputs(seed: int = 0) -> dict:
    key = jax.random.key(seed)
    ks = jax.random.split(key, 24)
    f32 = jnp.float32
    nrm = lambda k, shp, s: jax.random.normal(k, shp, f32) * s
    x = nrm(ks[0], (BATCH, SEQ, D_MODEL), 1.0)
    meta_tokens = nrm(ks[1], (N_META, D_MODEL), 1.0)
    ln0_g = 1.0 + nrm(ks[2], (D_MODEL,), 0.02)
    ln0_b = nrm(ks[3], (D_MODEL,), 0.02)
    w_in = nrm(ks[4], (DEPTH, D_MODEL, N_IN), D_MODEL ** -0.5)
    conv_w = nrm(ks[5], (DEPTH, CONV_K, D_CONV), CONV_K ** -0.5)
    conv_b = nrm(ks[6], (DEPTH, D_CONV), 0.02)
    dt0 = jnp.exp(jax.random.uniform(ks[7], (DEPTH, SSM_HEADS), f32, math.log(1e-3), math.log(1e-1)))
    dt_bias = dt0 + jnp.log(-jnp.expm1(-dt0))
    a_log = jnp.log(jax.random.uniform(ks[8], (DEPTH, SSM_HEADS), f32, 1.0, 16.0))
    d_skip = 1.0 + nrm(ks[9], (DEPTH, SSM_HEADS), 0.1)
    ssd_norm_w = 1.0 + nrm(ks[10], (DEPTH, D_SSM), 0.02)
    lambda_q1 = nrm(ks[11], (DEPTH, ATT_QK_DIM), 0.1)
    lambda_k1 = nrm(ks[12], (DEPTH, ATT_QK_DIM), 0.1)
    lambda_q2 = nrm(ks[13], (DEPTH, ATT_QK_DIM), 0.1)
    lambda_k2 = nrm(ks[14], (DEPTH, ATT_QK_DIM), 0.1)
    attn_norm_w = 1.0 + nrm(ks[15], (DEPTH, ATT_V_DIM), 0.02)
    w_out = nrm(ks[16], (DEPTH, D_MIX, D_MODEL), D_MIX ** -0.5 * BETA)
    ln1_g = 1.0 + nrm(ks[17], (DEPTH, D_MODEL), 0.02)
    ln1_b = nrm(ks[18], (DEPTH, D_MODEL), 0.02)
    w_up = nrm(ks[19], (DEPTH, D_MODEL, D_FF), D_MODEL ** -0.5)
    w_down = nrm(ks[20], (DEPTH, D_FF, D_MODEL), D_FF ** -0.5 * BETA)
    ln2_g = 1.0 + nrm(ks[21], (DEPTH, D_MODEL), 0.02)
    ln2_b = nrm(ks[22], (DEPTH, D_MODEL), 0.02)
    return {"x": x, "meta_tokens": meta_tokens, "ln0_g": ln0_g, "ln0_b": ln0_b,
            "w_in": w_in, "conv_w": conv_w, "conv_b": conv_b, "dt_bias": dt_bias,
            "a_log": a_log, "d_skip": d_skip, "ssd_norm_w": ssd_norm_w,
            "lambda_q1": lambda_q1, "lambda_k1": lambda_k1, "lambda_q2": lambda_q2,
            "lambda_k2": lambda_k2, "attn_norm_w": attn_norm_w, "w_out": w_out,
            "ln1_g": ln1_g, "ln1_b": ln1_b, "w_up": w_up, "w_down": w_down,
            "ln2_g": ln2_g, "ln2_b": ln2_b}


def reference(x, meta_tokens, ln0_g, ln0_b, w_in, conv_w, conv_b, dt_bias, a_log, d_skip,
              ssd_norm_w, lambda_q1, lambda_k1, lambda_q2, lambda_k2, attn_norm_w, w_out,
              ln1_g, ln1_b, w_up, w_down, ln2_g, ln2_b):
    b = x.shape[0]
    meta = jnp.broadcast_to(meta_tokens[None].astype(x.dtype), (b, N_META, D_MODEL))
    h = layer_norm(jnp.concatenate([meta, x], axis=1), ln0_g, ln0_b)
    for l in range(DEPTH):
        lambda_init = 0.8 - 0.6 * math.exp(-0.3 * l)
        mix = hybrid_mixer(h, w_in[l], conv_w[l], conv_b[l], dt_bias[l], a_log[l], d_skip[l],
                           ssd_norm_w[l], lambda_q1[l], lambda_k1[l], lambda_q2[l], lambda_k2[l],
                           attn_norm_w[l], w_out[l], lambda_init)
        h = layer_norm(ALPHA * h + mix, ln1_g[l], ln1_b[l])
        ff = jnp.square(jax.nn.relu(h @ w_up[l])) @ w_down[l]
        h = layer_norm(ALPHA * h + ff, ln2_g[l], ln2_b[l])
    return h[:, N_META:]
```

```python
import functools
import math

import jax
import jax.numpy as jnp
import numpy as np
from jax import lax
from jax.experimental import pallas as pl
from jax.experimental.pallas import tpu as pltpu

F32 = jnp.float32
BF16 = jnp.bfloat16

D_MODEL = 2048
N_META = 16
CHUNK = 128
PAD = CHUNK - N_META
D_SSM = 2048
D_ATT = 2048
SSM_HEAD_DIM = 64
SSM_HEADS = 32
SSM_GROUPS = 8
SSM_HPG = 4
GROUP_W = SSM_HPG * SSM_HEAD_DIM
D_STATE = 128
CONV_K = 4
D_CONV = D_SSM + 2 * SSM_GROUPS * D_STATE
ATT_V_DIM = 128
ATT_HEADS = 16
ATT_QK_DIM = 64
D_FF = 4 * D_MODEL
DEPTH = 1
ALPHA = (2 * DEPTH) ** 0.25
LN_EPS = 1e-5
RMS_EPS = 1e-5
NEG_INF = -1e30
LAMBDA_INIT = 0.8 - 0.6 * math.exp(-0.3 * 0)

N_PROJ = D_SSM + D_CONV + 3 * D_ATT
COL_Z = 0
COL_X = D_SSM
COL_B = COL_X + D_SSM
COL_C = COL_B + SSM_GROUPS * D_STATE
COL_Q = COL_C + SSM_GROUPS * D_STATE
COL_K = COL_Q + D_ATT
COL_V = COL_K + D_ATT
DT_ROWS = 8

V7X_VMEM_BYTES = 64 * 1024 * 1024
VMEM_LIMIT = 56 * 1024 * 1024

ATT_TQ = 256


def _layer_norm_rows(x, g, b):
    mu = jnp.mean(x, axis=-1, keepdims=True)
    xc = x - mu
    var = jnp.mean(xc * xc, axis=-1, keepdims=True)
    return xc * lax.rsqrt(var + LN_EPS) * g + b


def _silu(x):
    return x * (1.0 / (1.0 + jnp.exp(-x)))


def _softplus(x):
    return jnp.maximum(x, 0.0) + jnp.log1p(jnp.exp(-jnp.abs(x)))


def _inproj_kernel(x_ref, g_ref, b_ref, w_ref, wdt_ref, o_ref, dt_ref, hn_ref, *, ln_rows):
    j = pl.program_id(1)

    @pl.when(j == 0)
    def _():
        tm = x_ref.shape[0]

        def body(r, carry):
            r0 = pl.multiple_of(r * ln_rows, ln_rows)
            hn = _layer_norm_rows(x_ref[pl.ds(r0, ln_rows), :], g_ref[...], b_ref[...])
            hn_ref[pl.ds(r0, ln_rows), :] = hn.astype(BF16)
            return carry

        lax.fori_loop(0, tm // ln_rows, body, 0)
        dt_ref[...] = lax.dot_general(wdt_ref[...], hn_ref[...], (((1,), (1,)), ((), ())),
                                      preferred_element_type=F32)

    o_ref[...] = jnp.dot(hn_ref[...], w_ref[...], preferred_element_type=F32).astype(o_ref.dtype)


def _inproj(x2d, g, b, w, wdt, *, tm, tn):
    m = x2d.shape[0]
    n = w.shape[1]
    ndt = wdt.shape[0]
    ln_rows = min(tm, 128)
    return pl.pallas_call(
        functools.partial(_inproj_kernel, ln_rows=ln_rows),
        grid=(m // tm, n // tn),
        in_specs=[
            pl.BlockSpec((tm, D_MODEL), lambda i, j: (i, 0)),
            pl.BlockSpec((1, D_MODEL), lambda i, j: (0, 0)),
            pl.BlockSpec((1, D_MODEL), lambda i, j: (0, 0)),
            pl.BlockSpec((D_MODEL, tn), lambda i, j: (0, j)),
            pl.BlockSpec((ndt, D_MODEL), lambda i, j: (0, 0)),
        ],
        out_specs=[
            pl.BlockSpec((tm, tn), lambda i, j: (i, j)),
            pl.BlockSpec((ndt, tm), lambda i, j: (0, i)),
        ],
        out_shape=[
            jax.ShapeDtypeStruct((m, n), BF16),
            jax.ShapeDtypeStruct((ndt, m), F32),
        ],
        scratch_shapes=[pltpu.VMEM((tm, D_MODEL), BF16)],
        compiler_params=pltpu.CompilerParams(
            dimension_semantics=("arbitrary", "arbitrary"), vmem_limit_bytes=VMEM_LIMIT),
        name="inproj",
    )(x2d, g, b, w, wdt)


U_LEAD = 16
U_SEQ0 = U_LEAD + CHUNK


def _conv_silu(u_ref, r0, w_ref, b_ref):
    win = u_ref[pl.ds(r0 - 8, CHUNK + 8), :]
    w = w_ref[...]
    acc = b_ref[...] + w[3:4, :] * win[8:CHUNK + 8, :]
    acc = acc + w[2:3, :] * win[7:CHUNK + 7, :]
    acc = acc + w[1:2, :] * win[6:CHUNK + 6, :]
    acc = acc + w[0:1, :] * win[5:CHUNK + 5, :]
    return _silu(acc)


def _ssd_kernel(xs_ref, b_ref, c_ref, z_ref, xm_ref, bm_ref, cm_ref, dt_ref, dtm_ref,
                cwx_ref, cwb_ref, cwc_ref, cbx_ref, cbb_ref, cbc_ref,
                dtb_ref, alog_ref, dskip_ref, nw_ref,
                y_ref, ux_ref, ub_ref, uc_ref, st_ref):
    seq = xs_ref.shape[1]
    n_chunks = seq // CHUNK

    row = lax.broadcasted_iota(jnp.int32, (CHUNK, 1), 0)
    meta_valid = row >= PAD
    for u_ref, m_ref, s_ref in ((ux_ref, xm_ref, xs_ref), (ub_ref, bm_ref, b_ref), (uc_ref, cm_ref, c_ref)):
        ncol = u_ref.shape[1]
        u_ref[0:U_LEAD, :] = jnp.zeros((U_LEAD, ncol), F32)
        u_ref[U_LEAD:U_SEQ0, :] = jnp.where(meta_valid, m_ref[...].astype(F32), 0.0)
        u_ref[U_SEQ0:U_SEQ0 + seq, :] = s_ref[0].astype(F32)
    st_ref[...] = jnp.zeros(st_ref.shape, F32)

    dt_bias = dtb_ref[0]
    a_neg = -jnp.exp(alog_ref[0])

    def masks():
        li = lax.broadcasted_iota(jnp.int32, (CHUNK, CHUNK), 0)
        ki = lax.broadcasted_iota(jnp.int32, (CHUNK, CHUNK), 1)
        m1 = ki <= li
        tril = m1
        k2 = lax.broadcasted_iota(jnp.int32, (2 * CHUNK, 2 * CHUNK), 0) & (CHUNK - 1)
        s2 = lax.broadcasted_iota(jnp.int32, (2 * CHUNK, 2 * CHUNK), 1)
        m2cat = jnp.where((s2 >= CHUNK) | (k2 > s2), 1.0, 0.0).astype(BF16)
        lane = lax.broadcasted_iota(jnp.int32, (1, GROUP_W), 1) >> 6
        return m1, tril, m2cat, lane

    def chunk_step(r0, dt8, z_rows, out_row0):
        m1, tril, m2cat, lane = masks()
        xc = _conv_silu(ux_ref, r0, cwx_ref, cbx_ref)
        bc = _conv_silu(ub_ref, r0, cwb_ref, cbb_ref)
        a8 = dt8 * a_neg
        xs_bf = xc.astype(BF16)
        bt = bc.T
        st_old = st_ref[...]
        want_y = z_rows is not None
        if want_y:
            cc = _conv_silu(uc_ref, r0, cwc_ref, cbc_ref)
            cb = lax.dot_general(cc.astype(BF16), bc.astype(BF16), (((1,), (1,)), ((), ())),
                                 preferred_element_type=F32)
            st_bf = st_old.astype(BF16)
        lhs_y, rhs_y, lhs_s, rhs_s, tot = [], [], [], [], []
        for r in range(SSM_HPG):
            a_r = a8[r:r + 1, :]
            dt_r = dt8[r:r + 1, :]
            ma = jnp.where(m1, a_r, 0.0)
            hi = ma.astype(BF16)
            lo = (ma - hi.astype(F32)).astype(BF16)
            dd = jnp.dot(jnp.concatenate([hi, lo], axis=1), m2cat, preferred_element_type=F32)
            dseg = dd[:, :CHUNK]
            acsb = dd[:, CHUNK:]
            lmat = jnp.where(tril, jnp.exp(dseg), 0.0)
            head = lane == r
            xs_r = jnp.where(head, xs_bf, jnp.zeros_like(xs_bf))
            if want_y:
                lhs_y.append((cb * lmat * dt_r).astype(BF16))
                lhs_y.append((cc * jnp.exp(acsb)).astype(BF16))
                rhs_y.append(xs_r)
                rhs_y.append(jnp.where(head, st_bf, jnp.zeros_like(st_bf)))
            f1 = lmat[CHUNK - 1:CHUNK, :] * dt_r
            lhs_s.append((bt * f1).astype(BF16))
            rhs_s.append(xs_r)
            tot.append(acsb[CHUNK - 1:CHUNK, :])
        s_new = jnp.dot(jnp.concatenate(lhs_s, axis=1), jnp.concatenate(rhs_s, axis=0),
                        preferred_element_type=F32)
        half = lax.broadcasted_iota(jnp.int32, (1, CHUNK), 1) < SSM_HEAD_DIM
        decay = jnp.exp(jnp.concatenate([jnp.where(half, tot[0], tot[1]),
                                         jnp.where(half, tot[2], tot[3])], axis=1))
        st_ref[...] = st_old * decay + s_new
        if want_y:
            y = jnp.dot(jnp.concatenate(lhs_y, axis=1), jnp.concatenate(rhs_y, axis=0),
                        preferred_element_type=F32)
            y = y + xc * dskip_ref[...]
            gated = y * _silu(z_rows.astype(F32))
            ms = jnp.mean(gated * gated, axis=-1, keepdims=True)
            y_ref[0, pl.ds(out_row0, CHUNK), :] = (gated * lax.rsqrt(ms + RMS_EPS) * nw_ref[...]).astype(y_ref.dtype)

    lane_pos = lax.broadcasted_iota(jnp.int32, (1, CHUNK), 1)
    dt_meta = jnp.where(lane_pos >= PAD, _softplus(dtm_ref[0] + dt_bias), 0.0)
    chunk_step(U_LEAD, dt_meta, None, None)

    def body(s, carry):
        row0 = pl.multiple_of(s * CHUNK, CHUNK)
        dt8 = _softplus(dt_ref[0, :, pl.ds(row0, CHUNK)] + dt_bias)
        chunk_step(U_SEQ0 + row0, dt8, z_ref[0, pl.ds(row0, CHUNK), :], row0)
        return carry

    lax.fori_loop(0, n_chunks, body, 0)


def _ssd(proj3, proj_meta, dt3, dt3_meta, conv_w, conv_b2, dtb3, alog3, dskip_row, nw_row):
    nb, seq, _ = proj3.shape
    gx, gb, gc = COL_X // GROUP_W, COL_B // D_STATE, COL_C // D_STATE
    u_rows = U_SEQ0 + seq
    return pl.pallas_call(
        _ssd_kernel,
        grid=(nb, SSM_GROUPS),
        in_specs=[
            pl.BlockSpec((1, seq, GROUP_W), lambda b, g: (b, 0, gx + g)),
            pl.BlockSpec((1, seq, D_STATE), lambda b, g: (b, 0, gb + g)),
            pl.BlockSpec((1, seq, D_STATE), lambda b, g: (b, 0, gc + g)),
            pl.BlockSpec((1, seq, GROUP_W), lambda b, g: (b, 0, g)),
            pl.BlockSpec((CHUNK, GROUP_W), lambda b, g: (0, gx + g)),
            pl.BlockSpec((CHUNK, D_STATE), lambda b, g: (0, gb + g)),
            pl.BlockSpec((CHUNK, D_STATE), lambda b, g: (0, gc + g)),
            pl.BlockSpec((1, DT_ROWS, seq), lambda b, g: (g, 0, b)),
            pl.BlockSpec((1, DT_ROWS, CHUNK), lambda b, g: (g, 0, 0)),
            pl.BlockSpec((CONV_K, GROUP_W), lambda b, g: (0, g)),
            pl.BlockSpec((CONV_K, D_STATE), lambda b, g: (0, D_SSM // D_STATE + g)),
            pl.BlockSpec((CONV_K, D_STATE), lambda b, g: (0, D_SSM // D_STATE + SSM_GROUPS + g)),
            pl.BlockSpec((1, GROUP_W), lambda b, g: (0, g)),
            pl.BlockSpec((1, D_STATE), lambda b, g: (0, D_SSM // D_STATE + g)),
            pl.BlockSpec((1, D_STATE), lambda b, g: (0, D_SSM // D_STATE + SSM_GROUPS + g)),
            pl.BlockSpec((1, DT_ROWS, 1), lambda b, g: (g, 0, 0)),
            pl.BlockSpec((1, DT_ROWS, 1), lambda b, g: (g, 0, 0)),
            pl.BlockSpec((1, GROUP_W), lambda b, g: (0, g)),
            pl.BlockSpec((1, GROUP_W), lambda b, g: (0, g)),
        ],
        out_specs=pl.BlockSpec((1, seq, GROUP_W), lambda b, g: (b, 0, g)),
        out_shape=jax.ShapeDtypeStruct((nb, seq, D_SSM), BF16),
        scratch_shapes=[
            pltpu.VMEM((u_rows, GROUP_W), F32),
            pltpu.VMEM((u_rows, D_STATE), F32),
            pltpu.VMEM((u_rows, D_STATE), F32),
            pltpu.VMEM((D_STATE, GROUP_W), F32),
        ],
        compiler_params=pltpu.CompilerParams(
            dimension_semantics=("arbitrary", "arbitrary"), vmem_limit_bytes=VMEM_LIMIT),
        name="ssd",
    )(proj3, proj3, proj3, proj3, proj_meta, proj_meta, proj_meta, dt3, dt3_meta,
      conv_w, conv_w, conv_w, conv_b2, conv_b2, conv_b2, dtb3, alog3, dskip_row, nw_row)


def _attn_kernel(slope_ref, q_ref, k_ref, v_ref, km_ref, vm_ref, lam_ref, nw_ref, o_ref,
                 m_ref, l_ref, acc_ref):
    h = pl.program_id(1)
    qi = pl.program_id(2)
    tq = q_ref.shape[1]
    slope = slope_ref[h]

    q = q_ref[0] * jnp.asarray(ATT_QK_DIM ** -0.5, BF16)
    lane = lax.broadcasted_iota(jnp.int32, (tq, 2 * ATT_QK_DIM), 1)
    qs = (jnp.where(lane < ATT_QK_DIM, q, jnp.zeros_like(q)),
          jnp.where(lane >= ATT_QK_DIM, q, jnp.zeros_like(q)))

    def scores(c, k, bias):
        s = lax.dot_general(qs[c], k, (((1,), (1,)), ((), ())), preferred_element_type=F32)
        return s + bias

    def first_block(c, s, v):
        m = jnp.max(s, axis=-1, keepdims=True)
        p = jnp.exp(s - m)
        m_ref[c] = jnp.broadcast_to(m, (tq, 128))
        l_ref[c] = jnp.broadcast_to(jnp.sum(p, axis=-1, keepdims=True), (tq, 128))
        acc_ref[c] = jnp.dot(p.astype(BF16), v, preferred_element_type=F32)

    def next_block(c, s, v):
        m_prev = m_ref[c][:, :1]
        m_new = jnp.maximum(m_prev, jnp.max(s, axis=-1, keepdims=True))
        alpha = jnp.exp(m_prev - m_new)
        p = jnp.exp(s - m_new)
        m_ref[c] = jnp.broadcast_to(m_new, (tq, 128))
        l_ref[c] = alpha * l_ref[c] + jnp.sum(p, axis=-1, keepdims=True)
        acc_ref[c] = alpha * acc_ref[c] + jnp.dot(p.astype(BF16), v, preferred_element_type=F32)

    col_m = lax.broadcasted_iota(jnp.int32, (1, CHUNK), 1)
    bias_m = jnp.where(col_m >= PAD, slope * col_m.astype(F32), NEG_INF)
    for c in range(2):
        first_block(c, scores(c, km_ref[...], bias_m), vm_ref[...])

    col = lax.broadcasted_iota(jnp.int32, (1, tq), 1)

    def body(j, carry):
        r0 = pl.multiple_of(j * tq, tq)
        k = k_ref[0, pl.ds(r0, tq), :]
        v = v_ref[0, pl.ds(r0, tq), :]
        bias = slope * (col + (CHUNK + r0)).astype(F32)
        for c in range(2):
            next_block(c, scores(c, k, bias), v)
        return carry

    lax.fori_loop(0, qi, body, 0)

    r0 = pl.multiple_of(qi * tq, tq)
    k = k_ref[0, pl.ds(r0, tq), :]
    v = v_ref[0, pl.ds(r0, tq), :]
    rr = lax.broadcasted_iota(jnp.int32, (tq, tq), 0)
    cc = lax.broadcasted_iota(jnp.int32, (tq, tq), 1)
    bias = slope * (col + (CHUNK + r0)).astype(F32)
    for c in range(2):
        next_block(c, jnp.where(cc <= rr, scores(c, k, bias), NEG_INF), v)

    lam4 = lam_ref[...]
    lam = (jnp.exp(jnp.sum(lam4[0:1] * lam4[1:2], axis=-1, keepdims=True))
           - jnp.exp(jnp.sum(lam4[2:3] * lam4[3:4], axis=-1, keepdims=True)) + LAMBDA_INIT)
    o = acc_ref[0] / l_ref[0] - lam * (acc_ref[1] / l_ref[1])
    ms = jnp.mean(o * o, axis=-1, keepdims=True)
    o_ref[0] = (o * lax.rsqrt(ms + RMS_EPS) * nw_ref[...] * (1.0 - LAMBDA_INIT)).astype(o_ref.dtype)


def _attention(slopes, proj3, proj_meta, lam4, nw_row):
    nb, seq, _ = proj3.shape
    cq, ck, cv = COL_Q // ATT_V_DIM, COL_K // ATT_V_DIM, COL_V // ATT_V_DIM
    tq = ATT_TQ
    return pl.pallas_call(
        _attn_kernel,
        grid=(nb, ATT_HEADS, seq // tq),
        in_specs=[
            pl.BlockSpec(memory_space=pltpu.SMEM),
            pl.BlockSpec((1, tq, ATT_V_DIM), lambda b, h, i: (b, i, cq + h)),
            pl.BlockSpec((1, seq, ATT_V_DIM), lambda b, h, i: (b, 0, ck + h)),
            pl.BlockSpec((1, seq, ATT_V_DIM), lambda b, h, i: (b, 0, cv + h)),
            pl.BlockSpec((CHUNK, ATT_V_DIM), lambda b, h, i: (0, ck + h)),
            pl.BlockSpec((CHUNK, ATT_V_DIM), lambda b, h, i: (0, cv + h)),
            pl.BlockSpec((4, ATT_QK_DIM), lambda b, h, i: (0, 0)),
            pl.BlockSpec((1, ATT_V_DIM), lambda b, h, i: (0, 0)),
        ],
        out_specs=pl.BlockSpec((1, tq, ATT_V_DIM), lambda b, h, i: (b, i, h)),
        out_shape=jax.ShapeDtypeStruct((nb, seq, D_ATT), BF16),
        scratch_shapes=[
            pltpu.VMEM((2, tq, 128), F32),
            pltpu.VMEM((2, tq, 128), F32),
            pltpu.VMEM((2, tq, ATT_V_DIM), F32),
        ],
        compiler_params=pltpu.CompilerParams(
            dimension_semantics=("arbitrary", "arbitrary", "arbitrary"), vmem_limit_bytes=VMEM_LIMIT),
        name="diffattn",
    )(slopes, proj3, proj3, proj3, proj_meta, proj_meta, lam4, nw_row)


def _outproj_kernel(y_ref, o_ref, wy_ref, wo_ref, x_ref, g0_ref, b0_ref, g1_ref, b1_ref,
                    h_ref, acc_ref, *, ln_rows):
    j = pl.program_id(1)
    tn = wy_ref.shape[1]
    c0 = pl.multiple_of(j * tn, tn)
    acc_ref[:, pl.ds(c0, tn)] = (jnp.dot(y_ref[...], wy_ref[...], preferred_element_type=F32)
                                 + jnp.dot(o_ref[...], wo_ref[...], preferred_element_type=F32))

    @pl.when(j == pl.num_programs(1) - 1)
    def _():
        tm = x_ref.shape[0]

        def body(r, carry):
            r0 = pl.multiple_of(r * ln_rows, ln_rows)
            rows = pl.ds(r0, ln_rows)
            h0 = _layer_norm_rows(x_ref[rows, :], g0_ref[...], b0_ref[...])
            h_ref[rows, :] = _layer_norm_rows(ALPHA * h0 + acc_ref[rows, :], g1_ref[...], b1_ref[...])
            return carry

        lax.fori_loop(0, tm // ln_rows, body, 0)


def _outproj(y2d, o2d, w_out, x2d, g0, b0, g1, b1, *, tm, tn):
    m = x2d.shape[0]
    row = lambda i, j: (0, 0)
    return pl.pallas_call(
        functools.partial(_outproj_kernel, ln_rows=128),
        grid=(m // tm, D_MODEL // tn),
        in_specs=[
            pl.BlockSpec((tm, D_SSM), lambda i, j: (i, 0)),
            pl.BlockSpec((tm, D_ATT), lambda i, j: (i, 0)),
            pl.BlockSpec((D_SSM, tn), lambda i, j: (0, j)),
            pl.BlockSpec((D_ATT, tn), lambda i, j: (1, j)),
            pl.BlockSpec((tm, D_MODEL), lambda i, j: (i, 0)),
            pl.BlockSpec((1, D_MODEL), row),
            pl.BlockSpec((1, D_MODEL), row),
            pl.BlockSpec((1, D_MODEL), row),
            pl.BlockSpec((1, D_MODEL), row),
        ],
        out_specs=pl.BlockSpec((tm, D_MODEL), lambda i, j: (i, 0)),
        out_shape=jax.ShapeDtypeStruct((m, D_MODEL), F32),
        scratch_shapes=[pltpu.VMEM((tm, D_MODEL), F32)],
        compiler_params=pltpu.CompilerParams(
            dimension_semantics=("arbitrary", "arbitrary"), vmem_limit_bytes=VMEM_LIMIT),
        name="outproj",
    )(y2d, o2d, w_out, w_out, x2d, g0, b0, g1, b1)


def _mlp_kernel(h_ref, wu_ref, wd_ref, g_ref, b_ref, o_ref, hb_ref, acc_ref, *, ln_rows):
    f = pl.program_id(1)

    @pl.when(f == 0)
    def _():
        hb_ref[...] = h_ref[...].astype(BF16)

    u = jnp.dot(hb_ref[...], wu_ref[...], preferred_element_type=F32)
    u = jnp.square(jnp.maximum(u, 0.0)).astype(BF16)
    part = jnp.dot(u, wd_ref[...], preferred_element_type=F32)

    @pl.when(f == 0)
    def _():
        acc_ref[...] = part

    @pl.when(f > 0)
    def _():
        acc_ref[...] += part

    @pl.when(f == pl.num_programs(1) - 1)
    def _():
        tm = h_ref.shape[0]

        def body(r, carry):
            r0 = pl.multiple_of(r * ln_rows, ln_rows)
            rows = pl.ds(r0, ln_rows)
            o_ref[rows, :] = _layer_norm_rows(ALPHA * h_ref[rows, :] + acc_ref[rows, :], g_ref[...], b_ref[...])
            return carry

        lax.fori_loop(0, tm // ln_rows, body, 0)


def _mlp(h1, w_up, w_down, g, b, *, tm, tf):
    m = h1.shape[0]
    row = lambda i, f: (0, 0)
    return pl.pallas_call(
        functools.partial(_mlp_kernel, ln_rows=128),
        grid=(m // tm, D_FF // tf),
        in_specs=[
            pl.BlockSpec((tm, D_MODEL), lambda i, f: (i, 0)),
            pl.BlockSpec((D_MODEL, tf), lambda i, f: (0, f)),
            pl.BlockSpec((tf, D_MODEL), lambda i, f: (f, 0)),
            pl.BlockSpec((1, D_MODEL), row),
            pl.BlockSpec((1, D_MODEL), row),
        ],
        out_specs=pl.BlockSpec((tm, D_MODEL), lambda i, f: (i, 0)),
        out_shape=jax.ShapeDtypeStruct((m, D_MODEL), F32),
        scratch_shapes=[pltpu.VMEM((tm, D_MODEL), BF16), pltpu.VMEM((tm, D_MODEL), F32)],
        compiler_params=pltpu.CompilerParams(
            dimension_semantics=("arbitrary", "arbitrary"), vmem_limit_bytes=VMEM_LIMIT),
        name="mlp",
    )(h1, w_up, w_down, g, b)


def kernel(x, meta_tokens, ln0_g, ln0_b, w_in, conv_w, conv_b, dt_bias, a_log, d_skip, ssd_norm_w,
           lambda_q1, lambda_k1, lambda_q2, lambda_k2, attn_norm_w, w_out, ln1_g, ln1_b, w_up, w_down,
           ln2_g, ln2_b):
    nb, seq, d = x.shape
    assert (d, w_in.shape[0]) == (D_MODEL, DEPTH) and seq % ATT_TQ == 0
    m = nb * seq
    x2d = x.reshape(m, d)
    row = lambda v: v.reshape(1, -1).astype(F32)

    w_in0 = w_in[0]
    o_dt = D_SSM + D_CONV
    w_proj = jnp.concatenate([w_in0[:, :o_dt], w_in0[:, o_dt + SSM_HEADS:]], axis=1).astype(BF16)
    wdt = w_in0[:, o_dt:o_dt + SSM_HEADS].T.reshape(SSM_GROUPS, SSM_HPG, d)
    wdt = jnp.pad(wdt, ((0, 0), (0, DT_ROWS - SSM_HPG), (0, 0))).reshape(SSM_GROUPS * DT_ROWS, d).astype(BF16)
    pad_heads = lambda v: jnp.pad(v.reshape(SSM_GROUPS, SSM_HPG).astype(F32),
                                  ((0, 0), (0, DT_ROWS - SSM_HPG)))[..., None]
    dtb3, alog3 = pad_heads(dt_bias[0]), pad_heads(a_log[0])
    dskip_row = row(jnp.repeat(d_skip[0], SSM_HEAD_DIM))
    lam4 = jnp.stack([lambda_q1[0], lambda_k1[0], lambda_q2[0], lambda_k2[0]]).astype(F32)
    slopes = jnp.asarray(2.0 ** (-8.0 * np.arange(1, ATT_HEADS + 1) / ATT_HEADS), dtype=F32)
    w_out_b, w_up_b, w_down_b = w_out[0].astype(BF16), w_up[0].astype(BF16), w_down[0].astype(BF16)
    g0, b0 = row(ln0_g), row(ln0_b)
    meta_pad = jnp.pad(meta_tokens.astype(F32), ((PAD, 0), (0, 0)))

    proj, dt_t = _inproj(x2d, g0, b0, w_proj, wdt, tm=1024, tn=1024)
    proj_meta, dt_t_meta = _inproj(meta_pad, g0, b0, w_proj, wdt, tm=CHUNK, tn=1024)
    proj3 = proj.reshape(nb, seq, N_PROJ)
    dt3 = dt_t.reshape(SSM_GROUPS, DT_ROWS, m)
    dt3_meta = dt_t_meta.reshape(SSM_GROUPS, DT_ROWS, CHUNK)

    y = _ssd(proj3, proj_meta, dt3, dt3_meta, conv_w[0].astype(F32), row(conv_b[0]),
             dtb3, alog3, dskip_row, row(ssd_norm_w[0]))
    o = _attention(slopes, proj3, proj_meta, lam4, row(attn_norm_w[0]))

    h1 = _outproj(y.reshape(m, D_SSM), o.reshape(m, D_ATT), w_out_b, x2d, g0, b0,
                  row(ln1_g[0]), row(ln1_b[0]), tm=512, tn=512)
    h2 = _mlp(h1, w_up_b, w_down_b, row(ln2_g[0]), row(ln2_b[0]), tm=512, tf=1024)
    return h2.reshape(nb, seq, d)
```

```python
import functools
import math

import jax
import jax.numpy as jnp
import numpy as np
from jax import lax
from jax.experimental import pallas as pl
from jax.experimental.pallas import tpu as pltpu

F32 = jnp.float32
BF16 = jnp.bfloat16

D_MODEL = 2048
N_META = 16
CHUNK = 128
PAD = CHUNK - N_META
D_SSM = 2048
D_ATT = 2048
SSM_HEAD_DIM = 64
SSM_HEADS = 32
SSM_GROUPS = 8
SSM_HPG = 4
GROUP_W = SSM_HPG * SSM_HEAD_DIM
D_STATE = 128
CONV_K = 4
D_CONV = D_SSM + 2 * SSM_GROUPS * D_STATE
ATT_V_DIM = 128
ATT_HEADS = 16
ATT_QK_DIM = 64
D_FF = 4 * D_MODEL
DEPTH = 1
ALPHA = (2 * DEPTH) ** 0.25
LN_EPS = 1e-5
RMS_EPS = 1e-5
NEG_INF = -1e30
LAMBDA_INIT = 0.8 - 0.6 * math.exp(-0.3 * 0)

N_PROJ = D_SSM + D_CONV + 3 * D_ATT
COL_Z = 0
COL_X = D_SSM
COL_B = COL_X + D_SSM
COL_C = COL_B + SSM_GROUPS * D_STATE
COL_Q = COL_C + SSM_GROUPS * D_STATE
COL_K = COL_Q + D_ATT
COL_V = COL_K + D_ATT
DT_ROWS = 8

V7X_VMEM_BYTES = 64 * 1024 * 1024
VMEM_LIMIT = 56 * 1024 * 1024

ATT_TQ = 512


def _layer_norm_rows(x, g, b):
    mu = jnp.mean(x, axis=-1, keepdims=True)
    xc = x - mu
    var = jnp.mean(xc * xc, axis=-1, keepdims=True)
    return xc * lax.rsqrt(var + LN_EPS) * g + b


def _silu(x):
    return x * (1.0 / (1.0 + jnp.exp(-x)))


def _softplus(x):
    return jnp.maximum(x, 0.0) + jnp.log1p(jnp.exp(-jnp.abs(x)))


def _inproj_kernel(x_ref, g_ref, b_ref, w_ref, wdt_ref, o_ref, dt_ref, hn_ref, *, ln_rows):
    j = pl.program_id(1)

    @pl.when(j == 0)
    def _():
        tm = x_ref.shape[0]

        def body(r, carry):
            r0 = pl.multiple_of(r * ln_rows, ln_rows)
            hn = _layer_norm_rows(x_ref[pl.ds(r0, ln_rows), :], g_ref[...], b_ref[...])
            hn_ref[pl.ds(r0, ln_rows), :] = hn.astype(BF16)
            return carry

        lax.fori_loop(0, tm // ln_rows, body, 0)
        dt_ref[...] = lax.dot_general(wdt_ref[...], hn_ref[...], (((1,), (1,)), ((), ())),
                                      preferred_element_type=F32)

    o_ref[...] = jnp.dot(hn_ref[...], w_ref[...], preferred_element_type=F32).astype(o_ref.dtype)


def _inproj(x2d, g, b, w, wdt, *, tm, tn):
    m = x2d.shape[0]
    n = w.shape[1]
    ndt = wdt.shape[0]
    ln_rows = min(tm, 128)
    return pl.pallas_call(
        functools.partial(_inproj_kernel, ln_rows=ln_rows),
        grid=(m // tm, n // tn),
        in_specs=[
            pl.BlockSpec((tm, D_MODEL), lambda i, j: (i, 0)),
            pl.BlockSpec((1, D_MODEL), lambda i, j: (0, 0)),
            pl.BlockSpec((1, D_MODEL), lambda i, j: (0, 0)),
            pl.BlockSpec((D_MODEL, tn), lambda i, j: (0, j)),
            pl.BlockSpec((ndt, D_MODEL), lambda i, j: (0, 0)),
        ],
        out_specs=[
            pl.BlockSpec((tm, tn), lambda i, j: (i, j)),
            pl.BlockSpec((ndt, tm), lambda i, j: (0, i)),
        ],
        out_shape=[
            jax.ShapeDtypeStruct((m, n), BF16),
            jax.ShapeDtypeStruct((ndt, m), F32),
        ],
        scratch_shapes=[pltpu.VMEM((tm, D_MODEL), BF16)],
        compiler_params=pltpu.CompilerParams(
            dimension_semantics=("arbitrary", "arbitrary"), vmem_limit_bytes=VMEM_LIMIT),
        name="inproj",
    )(x2d, g, b, w, wdt)


U_LEAD = 16
U_SEQ0 = U_LEAD + CHUNK


def _conv_silu(u_ref, r0, w_ref, b_ref):
    win = u_ref[pl.ds(r0 - 8, CHUNK + 8), :]
    w = w_ref[...]
    acc = b_ref[...] + w[3:4, :] * win[8:CHUNK + 8, :]
    acc = acc + w[2:3, :] * win[7:CHUNK + 7, :]
    acc = acc + w[1:2, :] * win[6:CHUNK + 6, :]
    acc = acc + w[0:1, :] * win[5:CHUNK + 5, :]
    return _silu(acc)


def _ssd_kernel(xs_ref, b_ref, c_ref, z_ref, xm_ref, bm_ref, cm_ref, dt_ref, dtm_ref,
                cwx_ref, cwb_ref, cwc_ref, cbx_ref, cbb_ref, cbc_ref,
                dtb_ref, alog_ref, dskip_ref, nw_ref,
                y_ref, ux_ref, ub_ref, uc_ref, st_ref):
    seq = xs_ref.shape[1]
    n_chunks = seq // CHUNK

    row = lax.broadcasted_iota(jnp.int32, (CHUNK, 1), 0)
    meta_valid = row >= PAD
    for u_ref, m_ref, s_ref in ((ux_ref, xm_ref, xs_ref), (ub_ref, bm_ref, b_ref), (uc_ref, cm_ref, c_ref)):
        ncol = u_ref.shape[1]
        u_ref[0:U_LEAD, :] = jnp.zeros((U_LEAD, ncol), F32)
        u_ref[U_LEAD:U_SEQ0, :] = jnp.where(meta_valid, m_ref[...].astype(F32), 0.0)
        u_ref[U_SEQ0:U_SEQ0 + seq, :] = s_ref[0].astype(F32)
    st_ref[...] = jnp.zeros(st_ref.shape, F32)

    dt_bias = dtb_ref[0]
    a_neg = -jnp.exp(alog_ref[0])

    def masks():
        li = lax.broadcasted_iota(jnp.int32, (CHUNK, CHUNK), 0)
        ki = lax.broadcasted_iota(jnp.int32, (CHUNK, CHUNK), 1)
        m1 = ki <= li
        tril = m1
        k2 = lax.broadcasted_iota(jnp.int32, (2 * CHUNK, 2 * CHUNK), 0) & (CHUNK - 1)
        s2 = lax.broadcasted_iota(jnp.int32, (2 * CHUNK, 2 * CHUNK), 1)
        m2cat = jnp.where((s2 >= CHUNK) | (k2 > s2), 1.0, 0.0).astype(BF16)
        lane = lax.broadcasted_iota(jnp.int32, (1, GROUP_W), 1) >> 6
        return m1, tril, m2cat, lane

    def chunk_step(r0, dt8, z_rows, out_row0):
        m1, tril, m2cat, lane = masks()
        xc = _conv_silu(ux_ref, r0, cwx_ref, cbx_ref)
        bc = _conv_silu(ub_ref, r0, cwb_ref, cbb_ref)
        a8 = dt8 * a_neg
        xs_bf = xc.astype(BF16)
        bt = bc.T
        st_old = st_ref[...]
        want_y = z_rows is not None
        if want_y:
            cc = _conv_silu(uc_ref, r0, cwc_ref, cbc_ref)
            cb = lax.dot_general(cc.astype(BF16), bc.astype(BF16), (((1,), (1,)), ((), ())),
                                 preferred_element_type=F32)
            st_bf = st_old.astype(BF16)
        lhs_y, rhs_y, lhs_s, rhs_s, tot = [], [], [], [], []
        for r in range(SSM_HPG):
            a_r = a8[r:r + 1, :]
            dt_r = dt8[r:r + 1, :]
            ma = jnp.where(m1, a_r, 0.0)
            hi = ma.astype(BF16)
            lo = (ma - hi.astype(F32)).astype(BF16)
            dd = jnp.dot(jnp.concatenate([hi, lo], axis=1), m2cat, preferred_element_type=F32)
            dseg = dd[:, :CHUNK]
            acsb = dd[:, CHUNK:]
            lmat = jnp.where(tril, jnp.exp(dseg), 0.0)
            head = lane == r
            xs_r = jnp.where(head, xs_bf, jnp.zeros_like(xs_bf))
            if want_y:
                lhs_y.append((cb * lmat * dt_r).astype(BF16))
                lhs_y.append((cc * jnp.exp(acsb)).astype(BF16))
                rhs_y.append(xs_r)
                rhs_y.append(jnp.where(head, st_bf, jnp.zeros_like(st_bf)))
            f1 = lmat[CHUNK - 1:CHUNK, :] * dt_r
            lhs_s.append((bt * f1).astype(BF16))
            rhs_s.append(xs_r)
            tot.append(acsb[CHUNK - 1:CHUNK, :])
        s_new = jnp.dot(jnp.concatenate(lhs_s, axis=1), jnp.concatenate(rhs_s, axis=0),
                        preferred_element_type=F32)
        half = lax.broadcasted_iota(jnp.int32, (1, CHUNK), 1) < SSM_HEAD_DIM
        decay = jnp.exp(jnp.concatenate([jnp.where(half, tot[0], tot[1]),
                                         jnp.where(half, tot[2], tot[3])], axis=1))
        st_ref[...] = st_old * decay + s_new
        if want_y:
            y = jnp.dot(jnp.concatenate(lhs_y, axis=1), jnp.concatenate(rhs_y, axis=0),
                        preferred_element_type=F32)
            y = y + xc * dskip_ref[...]
            gated = y * _silu(z_rows.astype(F32))
            ms = jnp.mean(gated * gated, axis=-1, keepdims=True)
            y_ref[0, pl.ds(out_row0, CHUNK), :] = (gated * lax.rsqrt(ms + RMS_EPS) * nw_ref[...]).astype(y_ref.dtype)

    lane_pos = lax.broadcasted_iota(jnp.int32, (1, CHUNK), 1)
    dt_meta = jnp.where(lane_pos >= PAD, _softplus(dtm_ref[0] + dt_bias), 0.0)
    chunk_step(U_LEAD, dt_meta, None, None)

    def body(s, carry):
        row0 = pl.multiple_of(s * CHUNK, CHUNK)
        dt8 = _softplus(dt_ref[0, :, pl.ds(row0, CHUNK)] + dt_bias)
        chunk_step(U_SEQ0 + row0, dt8, z_ref[0, pl.ds(row0, CHUNK), :], row0)
        return carry

    lax.fori_loop(0, n_chunks, body, 0, unroll=2)


def _ssd(proj3, proj_meta, dt3, dt3_meta, conv_w, conv_b2, dtb3, alog3, dskip_row, nw_row):
    nb, seq, _ = proj3.shape
    gx, gb, gc = COL_X // GROUP_W, COL_B // D_STATE, COL_C // D_STATE
    u_rows = U_SEQ0 + seq
    return pl.pallas_call(
        _ssd_kernel,
        grid=(nb, SSM_GROUPS),
        in_specs=[
            pl.BlockSpec((1, seq, GROUP_W), lambda b, g: (b, 0, gx + g)),
            pl.BlockSpec((1, seq, D_STATE), lambda b, g: (b, 0, gb + g)),
            pl.BlockSpec((1, seq, D_STATE), lambda b, g: (b, 0, gc + g)),
            pl.BlockSpec((1, seq, GROUP_W), lambda b, g: (b, 0, g)),
            pl.BlockSpec((CHUNK, GROUP_W), lambda b, g: (0, gx + g)),
            pl.BlockSpec((CHUNK, D_STATE), lambda b, g: (0, gb + g)),
            pl.BlockSpec((CHUNK, D_STATE), lambda b, g: (0, gc + g)),
            pl.BlockSpec((1, DT_ROWS, seq), lambda b, g: (g, 0, b)),
            pl.BlockSpec((1, DT_ROWS, CHUNK), lambda b, g: (g, 0, 0)),
            pl.BlockSpec((CONV_K, GROUP_W), lambda b, g: (0, g)),
            pl.BlockSpec((CONV_K, D_STATE), lambda b, g: (0, D_SSM // D_STATE + g)),
            pl.BlockSpec((CONV_K, D_STATE), lambda b, g: (0, D_SSM // D_STATE + SSM_GROUPS + g)),
            pl.BlockSpec((1, GROUP_W), lambda b, g: (0, g)),
            pl.BlockSpec((1, D_STATE), lambda b, g: (0, D_SSM // D_STATE + g)),
            pl.BlockSpec((1, D_STATE), lambda b, g: (0, D_SSM // D_STATE + SSM_GROUPS + g)),
            pl.BlockSpec((1, DT_ROWS, 1), lambda b, g: (g, 0, 0)),
            pl.BlockSpec((1, DT_ROWS, 1), lambda b, g: (g, 0, 0)),
            pl.BlockSpec((1, GROUP_W), lambda b, g: (0, g)),
            pl.BlockSpec((1, GROUP_W), lambda b, g: (0, g)),
        ],
        out_specs=pl.BlockSpec((1, seq, GROUP_W), lambda b, g: (b, 0, g)),
        out_shape=jax.ShapeDtypeStruct((nb, seq, D_SSM), BF16),
        scratch_shapes=[
            pltpu.VMEM((u_rows, GROUP_W), F32),
            pltpu.VMEM((u_rows, D_STATE), F32),
            pltpu.VMEM((u_rows, D_STATE), F32),
            pltpu.VMEM((D_STATE, GROUP_W), F32),
        ],
        compiler_params=pltpu.CompilerParams(
            dimension_semantics=("arbitrary", "arbitrary"), vmem_limit_bytes=VMEM_LIMIT),
        name="ssd",
    )(proj3, proj3, proj3, proj3, proj_meta, proj_meta, proj_meta, dt3, dt3_meta,
      conv_w, conv_w, conv_w, conv_b2, conv_b2, conv_b2, dtb3, alog3, dskip_row, nw_row)


def _attn_kernel(slope_ref, q_ref, k_ref, v_ref, km_ref, vm_ref, lam_ref, nw_ref, o_ref,
                 kaug_ref, vt_ref, s_ref, p_ref):
    h = pl.program_id(1)
    seq = q_ref.shape[1]
    tq = ATT_TQ
    slope = slope_ref[h]

    def stage(r0, k, v, valid_from):
        nrows = k.shape[0]
        pos = lax.broadcasted_iota(jnp.int32, (nrows, 128), 0) + r0
        bias = slope * pos.astype(F32)
        if valid_from:
            bias = jnp.where(pos >= valid_from, bias, NEG_INF)
        b1 = bias.astype(BF16).astype(F32)
        b2 = (bias - b1).astype(BF16).astype(F32)
        b3 = bias - b1 - b2
        lane = lax.broadcasted_iota(jnp.int32, (nrows, 128), 1)
        cols = jnp.where(lane == 0, b1, jnp.where(lane == 1, b2, jnp.where(lane == 2, b3, 0.0)))
        kaug_ref[r0:r0 + nrows, 0:128] = k
        kaug_ref[r0:r0 + nrows, 128:256] = cols.astype(BF16)
        vt_ref[:, r0:r0 + nrows] = v.astype(F32).T.astype(BF16)

    stage(0, km_ref[...], vm_ref[...], PAD)
    for j in range(seq // CHUNK):
        stage(CHUNK * (j + 1), k_ref[0, CHUNK * j:CHUNK * (j + 1), :], v_ref[0, CHUNK * j:CHUNK * (j + 1), :], 0)

    lam4 = lam_ref[...]
    lam = (jnp.exp(jnp.sum(lam4[0:1] * lam4[1:2], axis=-1, keepdims=True))
           - jnp.exp(jnp.sum(lam4[2:3] * lam4[3:4], axis=-1, keepdims=True)) + LAMBDA_INIT)

    lane_q = lax.broadcasted_iota(jnp.int32, (tq, 128), 1)
    ones_blk = jnp.where(lane_q < 3, 1.0, 0.0).astype(BF16)
    map_lanes = (lane_q < ATT_QK_DIM, lane_q >= ATT_QK_DIM)
    causal = (lax.broadcasted_iota(jnp.int32, (tq, tq), 0)
              <= lax.broadcasted_iota(jnp.int32, (tq, tq), 1))

    for t in range(seq // tq):
        kend = CHUNK + tq * (t + 1)
        blocks = [(0, CHUNK)] + [(CHUNK + tq * j, CHUNK + tq * (j + 1)) for j in range(t + 1)]
        q = q_ref[0, tq * t:tq * (t + 1), :] * jnp.asarray(ATT_QK_DIM ** -0.5, BF16)
        outs = []
        for c in range(2):
            qaug = jnp.concatenate([jnp.where(map_lanes[c], q, jnp.zeros_like(q)), ones_blk], axis=1)
            m = None
            for r0, r1 in blocks:
                sv = lax.dot_general(kaug_ref[r0:r1, :], qaug, (((1,), (1,)), ((), ())),
                                     preferred_element_type=F32)
                if r1 == kend:
                    sv = jnp.where(causal, sv, NEG_INF)
                s_ref[c, r0:r1, :] = sv
                bm = jnp.max(sv, axis=0, keepdims=True)
                m = bm if m is None else jnp.maximum(m, bm)
            l = None
            for r0, r1 in blocks:
                p = jnp.exp(s_ref[c, r0:r1, :] - m)
                bl = jnp.sum(p, axis=0, keepdims=True)
                l = bl if l is None else l + bl
                p_ref[c, r0:r1, :] = p.astype(BF16)
            acc = jnp.dot(vt_ref[:, 0:kend], p_ref[c, 0:kend, :], preferred_element_type=F32)
            outs.append(acc / l)
        o = outs[0] - lam * outs[1]
        ms = jnp.mean(o * o, axis=0, keepdims=True)
        o = o * lax.rsqrt(ms + RMS_EPS) * nw_ref[...] * (1.0 - LAMBDA_INIT)
        o_ref[0, tq * t:tq * (t + 1), :] = o.T.astype(o_ref.dtype)


def _attention(slopes, proj3, proj_meta, lam4, nw_col):
    nb, seq, _ = proj3.shape
    cq, ck, cv = COL_Q // ATT_V_DIM, COL_K // ATT_V_DIM, COL_V // ATT_V_DIM
    kv_rows = CHUNK + seq
    return pl.pallas_call(
        _attn_kernel,
        grid=(nb, ATT_HEADS),
        in_specs=[
            pl.BlockSpec(memory_space=pltpu.SMEM),
            pl.BlockSpec((1, seq, ATT_V_DIM), lambda b, h: (b, 0, cq + h)),
            pl.BlockSpec((1, seq, ATT_V_DIM), lambda b, h: (b, 0, ck + h)),
            pl.BlockSpec((1, seq, ATT_V_DIM), lambda b, h: (b, 0, cv + h)),
            pl.BlockSpec((CHUNK, ATT_V_DIM), lambda b, h: (0, ck + h)),
            pl.BlockSpec((CHUNK, ATT_V_DIM), lambda b, h: (0, cv + h)),
            pl.BlockSpec((4, ATT_QK_DIM), lambda b, h: (0, 0)),
            pl.BlockSpec((ATT_V_DIM, 1), lambda b, h: (0, 0)),
        ],
        out_specs=pl.BlockSpec((1, seq, ATT_V_DIM), lambda b, h: (b, 0, h)),
        out_shape=jax.ShapeDtypeStruct((nb, seq, D_ATT), BF16),
        scratch_shapes=[
            pltpu.VMEM((kv_rows, 2 * ATT_V_DIM), BF16),
            pltpu.VMEM((ATT_V_DIM, kv_rows), BF16),
            pltpu.VMEM((2, kv_rows, ATT_TQ), F32),
            pltpu.VMEM((2, kv_rows, ATT_TQ), BF16),
        ],
        compiler_params=pltpu.CompilerParams(
            dimension_semantics=("arbitrary", "arbitrary"), vmem_limit_bytes=VMEM_LIMIT),
        name="diffattn",
    )(slopes, proj3, proj3, proj3, proj_meta, proj_meta, lam4, nw_col)


def _outproj_kernel(y_ref, o_ref, wy_ref, wo_ref, x_ref, g0_ref, b0_ref, g1_ref, b1_ref,
                    h_ref, acc_ref, *, ln_rows):
    j = pl.program_id(1)
    tn = wy_ref.shape[1]
    c0 = pl.multiple_of(j * tn, tn)
    acc_ref[:, pl.ds(c0, tn)] = (jnp.dot(y_ref[...], wy_ref[...], preferred_element_type=F32)
                                 + jnp.dot(o_ref[...], wo_ref[...], preferred_element_type=F32))

    @pl.when(j == pl.num_programs(1) - 1)
    def _():
        tm = x_ref.shape[0]

        def body(r, carry):
            r0 = pl.multiple_of(r * ln_rows, ln_rows)
            rows = pl.ds(r0, ln_rows)
            h0 = _layer_norm_rows(x_ref[rows, :], g0_ref[...], b0_ref[...])
            h_ref[rows, :] = _layer_norm_rows(ALPHA * h0 + acc_ref[rows, :], g1_ref[...], b1_ref[...])
            return carry

        lax.fori_loop(0, tm // ln_rows, body, 0)


def _outproj(y2d, o2d, w_out, x2d, g0, b0, g1, b1, *, tm, tn):
    m = x2d.shape[0]
    row = lambda i, j: (0, 0)
    return pl.pallas_call(
        functools.partial(_outproj_kernel, ln_rows=128),
        grid=(m // tm, D_MODEL // tn),
        in_specs=[
            pl.BlockSpec((tm, D_SSM), lambda i, j: (i, 0)),
            pl.BlockSpec((tm, D_ATT), lambda i, j: (i, 0)),
            pl.BlockSpec((D_SSM, tn), lambda i, j: (0, j)),
            pl.BlockSpec((D_ATT, tn), lambda i, j: (1, j)),
            pl.BlockSpec((tm, D_MODEL), lambda i, j: (i, 0)),
            pl.BlockSpec((1, D_MODEL), row),
            pl.BlockSpec((1, D_MODEL), row),
            pl.BlockSpec((1, D_MODEL), row),
            pl.BlockSpec((1, D_MODEL), row),
        ],
        out_specs=pl.BlockSpec((tm, D_MODEL), lambda i, j: (i, 0)),
        out_shape=jax.ShapeDtypeStruct((m, D_MODEL), F32),
        scratch_shapes=[pltpu.VMEM((tm, D_MODEL), F32)],
        compiler_params=pltpu.CompilerParams(
            dimension_semantics=("arbitrary", "arbitrary"), vmem_limit_bytes=VMEM_LIMIT),
        name="outproj",
    )(y2d, o2d, w_out, w_out, x2d, g0, b0, g1, b1)


def _mlp_kernel(h_ref, wu_ref, wd_ref, g_ref, b_ref, o_ref, hb_ref, acc_ref, *, ln_rows):
    f = pl.program_id(1)

    @pl.when(f == 0)
    def _():
        hb_ref[...] = h_ref[...].astype(BF16)

    u = jnp.dot(hb_ref[...], wu_ref[...], preferred_element_type=F32)
    u = jnp.square(jnp.maximum(u, 0.0)).astype(BF16)
    part = jnp.dot(u, wd_ref[...], preferred_element_type=F32)

    @pl.when(f == 0)
    def _():
        acc_ref[...] = part

    @pl.when(f > 0)
    def _():
        acc_ref[...] += part

    @pl.when(f == pl.num_programs(1) - 1)
    def _():
        tm = h_ref.shape[0]

        def body(r, carry):
            r0 = pl.multiple_of(r * ln_rows, ln_rows)
            rows = pl.ds(r0, ln_rows)
            o_ref[rows, :] = _layer_norm_rows(ALPHA * h_ref[rows, :] + acc_ref[rows, :], g_ref[...], b_ref[...])
            return carry

        lax.fori_loop(0, tm // ln_rows, body, 0)


def _mlp(h1, w_up, w_down, g, b, *, tm, tf):
    m = h1.shape[0]
    row = lambda i, f: (0, 0)
    return pl.pallas_call(
        functools.partial(_mlp_kernel, ln_rows=128),
        grid=(m // tm, D_FF // tf),
        in_specs=[
            pl.BlockSpec((tm, D_MODEL), lambda i, f: (i, 0)),
            pl.BlockSpec((D_MODEL, tf), lambda i, f: (0, f)),
            pl.BlockSpec((tf, D_MODEL), lambda i, f: (f, 0)),
            pl.BlockSpec((1, D_MODEL), row),
            pl.BlockSpec((1, D_MODEL), row),
        ],
        out_specs=pl.BlockSpec((tm, D_MODEL), lambda i, f: (i, 0)),
        out_shape=jax.ShapeDtypeStruct((m, D_MODEL), F32),
        scratch_shapes=[pltpu.VMEM((tm, D_MODEL), BF16), pltpu.VMEM((tm, D_MODEL), F32)],
        compiler_params=pltpu.CompilerParams(
            dimension_semantics=("arbitrary", "arbitrary"), vmem_limit_bytes=VMEM_LIMIT),
        name="mlp",
    )(h1, w_up, w_down, g, b)


def kernel(x, meta_tokens, ln0_g, ln0_b, w_in, conv_w, conv_b, dt_bias, a_log, d_skip, ssd_norm_w,
           lambda_q1, lambda_k1, lambda_q2, lambda_k2, attn_norm_w, w_out, ln1_g, ln1_b, w_up, w_down,
           ln2_g, ln2_b):
    nb, seq, d = x.shape
    assert (d, w_in.shape[0]) == (D_MODEL, DEPTH) and seq % ATT_TQ == 0
    m = nb * seq
    x2d = x.reshape(m, d)
    row = lambda v: v.reshape(1, -1).astype(F32)

    w_in0 = w_in[0]
    o_dt = D_SSM + D_CONV
    w_proj = jnp.concatenate([w_in0[:, :o_dt], w_in0[:, o_dt + SSM_HEADS:]], axis=1).astype(BF16)
    wdt = w_in0[:, o_dt:o_dt + SSM_HEADS].T.reshape(SSM_GROUPS, SSM_HPG, d)
    wdt = jnp.pad(wdt, ((0, 0), (0, DT_ROWS - SSM_HPG), (0, 0))).reshape(SSM_GROUPS * DT_ROWS, d).astype(BF16)
    pad_heads = lambda v: jnp.pad(v.reshape(SSM_GROUPS, SSM_HPG).astype(F32),
                                  ((0, 0), (0, DT_ROWS - SSM_HPG)))[..., None]
    dtb3, alog3 = pad_heads(dt_bias[0]), pad_heads(a_log[0])
    dskip_row = row(jnp.repeat(d_skip[0], SSM_HEAD_DIM))
    lam4 = jnp.stack([lambda_q1[0], lambda_k1[0], lambda_q2[0], lambda_k2[0]]).astype(F32)
    slopes = jnp.asarray(2.0 ** (-8.0 * np.arange(1, ATT_HEADS + 1) / ATT_HEADS), dtype=F32)
    w_out_b, w_up_b, w_down_b = w_out[0].astype(BF16), w_up[0].astype(BF16), w_down[0].astype(BF16)
    g0, b0 = row(ln0_g), row(ln0_b)
    meta_pad = jnp.pad(meta_tokens.astype(F32), ((PAD, 0), (0, 0)))

    proj, dt_t = _inproj(x2d, g0, b0, w_proj, wdt, tm=1024, tn=1024)
    proj_meta, dt_t_meta = _inproj(meta_pad, g0, b0, w_proj, wdt, tm=CHUNK, tn=1024)
    proj3 = proj.reshape(nb, seq, N_PROJ)
    dt3 = dt_t.reshape(SSM_GROUPS, DT_ROWS, m)
    dt3_meta = dt_t_meta.reshape(SSM_GROUPS, DT_ROWS, CHUNK)

    y = _ssd(proj3, proj_meta, dt3, dt3_meta, conv_w[0].astype(F32), row(conv_b[0]),
             dtb3, alog3, dskip_row, row(ssd_norm_w[0]))
    o = _attention(slopes, proj3, proj_meta, lam4, attn_norm_w[0].reshape(ATT_V_DIM, 1).astype(F32))

    h1 = _outproj(y.reshape(m, D_SSM), o.reshape(m, D_ATT), w_out_b, x2d, g0, b0,
                  row(ln1_g[0]), row(ln1_b[0]), tm=512, tn=512)
    h2 = _mlp(h1, w_up_b, w_down_b, row(ln2_g[0]), row(ln2_b[0]), tm=512, tf=1024)
    return h2.reshape(nb, seq, d)
```

```python
import functools
import math

import jax
import jax.numpy as jnp
import numpy as np
from jax import lax
from jax.experimental import pallas as pl
from jax.experimental.pallas import tpu as pltpu

F32 = jnp.float32
BF16 = jnp.bfloat16

D_MODEL = 2048
N_META = 16
CHUNK = 128
PAD = CHUNK - N_META
D_SSM = 2048
D_ATT = 2048
SSM_HEAD_DIM = 64
SSM_HEADS = 32
SSM_GROUPS = 8
SSM_HPG = 4
GROUP_W = SSM_HPG * SSM_HEAD_DIM
D_STATE = 128
CONV_K = 4
D_CONV = D_SSM + 2 * SSM_GROUPS * D_STATE
ATT_V_DIM = 128
ATT_HEADS = 16
ATT_QK_DIM = 64
D_FF = 4 * D_MODEL
DEPTH = 1
ALPHA = (2 * DEPTH) ** 0.25
LN_EPS = 1e-5
RMS_EPS = 1e-5
NEG_INF = -1e30
LAMBDA_INIT = 0.8 - 0.6 * math.exp(-0.3 * 0)

N_PROJ = D_SSM + D_CONV + 3 * D_ATT
COL_Z = 0
COL_X = D_SSM
COL_B = COL_X + D_SSM
COL_C = COL_B + SSM_GROUPS * D_STATE
COL_Q = COL_C + SSM_GROUPS * D_STATE
COL_K = COL_Q + D_ATT
COL_V = COL_K + D_ATT
W_IN_DT = D_SSM + D_CONV
DT_ROWS = 8

V7X_VMEM_BYTES = 64 * 1024 * 1024
VMEM_LIMIT = 56 * 1024 * 1024

ATT_TQ = 512
MLP_ACC_SLAB = 512


def _layer_norm_rows(x, g, b):
    mu = jnp.mean(x, axis=-1, keepdims=True)
    xc = x - mu
    var = jnp.mean(xc * xc, axis=-1, keepdims=True)
    return xc * lax.rsqrt(var + LN_EPS) * g + b


def _silu(x):
    hx = 0.5 * x
    return hx + hx * jnp.tanh(hx)


def _softplus(x):
    return jnp.maximum(x, 0.0) + jnp.log1p(jnp.exp(-jnp.abs(x)))


def _inproj_kernel(x_ref, xm_ref, g_ref, b_ref, wt_ref, wdt_ref, o_ref, dt_ref, om_ref, dtm_ref,
                   hn_ref, hm_ref, *, ln_rows):
    i = pl.program_id(0)
    j = pl.program_id(1)
    last_i = i == pl.num_programs(0) - 1
    contract_model = (((1,), (1,)), ((), ()))

    @pl.when(j == 0)
    def _():
        tm = x_ref.shape[0]

        def body(r, carry):
            r0 = pl.multiple_of(r * ln_rows, ln_rows)
            hn = _layer_norm_rows(x_ref[pl.ds(r0, ln_rows), :], g_ref[...], b_ref[...])
            hn_ref[pl.ds(r0, ln_rows), :] = hn.astype(BF16)
            return carry

        lax.fori_loop(0, tm // ln_rows, body, 0)
        dt_ref[...] = lax.dot_general(wdt_ref[...], hn_ref[...], contract_model, preferred_element_type=F32)

    @pl.when(last_i & (j == 0))
    def _():
        hm_ref[...] = _layer_norm_rows(xm_ref[...], g_ref[...], b_ref[...]).astype(BF16)
        dtm_ref[...] = lax.dot_general(wdt_ref[...], hm_ref[...], contract_model, preferred_element_type=F32)

    w = wt_ref[...].reshape(o_ref.shape[1], wt_ref.shape[2]).astype(BF16)
    o_ref[...] = lax.dot_general(hn_ref[...], w, contract_model, preferred_element_type=F32).astype(o_ref.dtype)

    @pl.when(last_i)
    def _():
        om_ref[...] = lax.dot_general(hm_ref[...], w, contract_model,
                                      preferred_element_type=F32).astype(om_ref.dtype)


def _inproj(x2d, meta_pad, g, b, wt, wdt, *, tm, tn):
    m = x2d.shape[0]
    n = N_PROJ
    ndt = wdt.shape[0]
    n_i = m // tm
    assert W_IN_DT % tn == 0 and n % tn == 0 and tn % SSM_HEADS == 0 and wt.shape[1] == SSM_HEADS
    const = lambda i, j: (0, 0)
    groups = tn // SSM_HEADS
    group_start = lambda j: j * groups + (j >= W_IN_DT // tn).astype(jnp.int32)
    return pl.pallas_call(
        functools.partial(_inproj_kernel, ln_rows=128),
        grid=(n_i, n // tn),
        in_specs=[
            pl.BlockSpec((tm, D_MODEL), lambda i, j: (i, 0)),
            pl.BlockSpec((CHUNK, D_MODEL), const),
            pl.BlockSpec((1, D_MODEL), const),
            pl.BlockSpec((1, D_MODEL), const),
            pl.BlockSpec((pl.Element(groups), pl.Element(SSM_HEADS), pl.Element(D_MODEL)),
                         lambda i, j: (group_start(j), 0, 0)),
            pl.BlockSpec((ndt, D_MODEL), const),
        ],
        out_specs=[
            pl.BlockSpec((tm, tn), lambda i, j: (i, j)),
            pl.BlockSpec((ndt, tm), lambda i, j: (0, i)),
            pl.BlockSpec((CHUNK, tn), lambda i, j: (0, jnp.where(i == n_i - 1, j, 0))),
            pl.BlockSpec((ndt, CHUNK), const),
        ],
        out_shape=[
            jax.ShapeDtypeStruct((m, n), BF16),
            jax.ShapeDtypeStruct((ndt, m), F32),
            jax.ShapeDtypeStruct((CHUNK, n), BF16),
            jax.ShapeDtypeStruct((ndt, CHUNK), F32),
        ],
        scratch_shapes=[pltpu.VMEM((tm, D_MODEL), BF16), pltpu.VMEM((CHUNK, D_MODEL), BF16)],
        compiler_params=pltpu.CompilerParams(
            dimension_semantics=("arbitrary", "arbitrary"), vmem_limit_bytes=VMEM_LIMIT),
        name="inproj",
    )(x2d, meta_pad, g, b, wt, wdt)


U_LEAD = 16
U_SEQ0 = U_LEAD + CHUNK


def _conv_silu(u_ref, r0, w_ref, b_ref):
    win = u_ref[pl.ds(r0 - 8, CHUNK + 8), :]
    w = w_ref[...]
    acc = b_ref[...] + w[3:4, :] * win[8:CHUNK + 8, :]
    acc = acc + w[2:3, :] * win[7:CHUNK + 7, :]
    acc = acc + w[1:2, :] * win[6:CHUNK + 6, :]
    acc = acc + w[0:1, :] * win[5:CHUNK + 5, :]
    return _silu(acc)


def _ssd_kernel(xs_ref, b_ref, c_ref, z_ref, xm_ref, bm_ref, cm_ref, dt_ref, dtm_ref,
                cwx_ref, cwb_ref, cwc_ref, cbx_ref, cbb_ref, cbc_ref,
                dtb_ref, alog_ref, dskip_ref, nw_ref,
                y_ref, ux_ref, ub_ref, uc_ref, st_ref, m1_ref, tril_ref, m2_ref, hm_ref):
    seq = xs_ref.shape[1]
    n_chunks = seq // CHUNK

    li = lax.broadcasted_iota(jnp.int32, (CHUNK, CHUNK), 0)
    ki = lax.broadcasted_iota(jnp.int32, (CHUNK, CHUNK), 1)
    lower = jnp.where(ki <= li, 1.0, 0.0)
    tril_ref[...] = lower
    m1_ref[...] = lower.astype(BF16)
    k2 = lax.broadcasted_iota(jnp.int32, (2 * CHUNK, 2 * CHUNK), 0) & (CHUNK - 1)
    s2 = lax.broadcasted_iota(jnp.int32, (2 * CHUNK, 2 * CHUNK), 1)
    m2_ref[...] = jnp.where((s2 >= CHUNK) | (k2 > s2), 1.0, 0.0).astype(BF16)
    head_of = lax.broadcasted_iota(jnp.int32, (CHUNK, GROUP_W), 1) >> 6
    for r in range(SSM_HPG):
        hm_ref[r] = jnp.where(head_of == r, 1.0, 0.0).astype(BF16)

    row = lax.broadcasted_iota(jnp.int32, (CHUNK, 1), 0)
    meta_valid = row >= PAD
    for u_ref, m_ref, s_ref in ((ux_ref, xm_ref, xs_ref), (ub_ref, bm_ref, b_ref), (uc_ref, cm_ref, c_ref)):
        ncol = u_ref.shape[1]
        u_ref[0:U_LEAD, :] = jnp.zeros((U_LEAD, ncol), F32)
        u_ref[U_LEAD:U_SEQ0, :] = jnp.where(meta_valid, m_ref[...].astype(F32), 0.0)
        u_ref[U_SEQ0:U_SEQ0 + seq, :] = s_ref[0].astype(F32)
    st_ref[...] = jnp.zeros(st_ref.shape, F32)

    dt_bias = dtb_ref[0]
    a_neg = -jnp.exp(alog_ref[0])

    def chunk_step(r0, dt8, z_rows, out_row0):
        xc = _conv_silu(ux_ref, r0, cwx_ref, cbx_ref)
        bc = _conv_silu(ub_ref, r0, cwb_ref, cbb_ref)
        a8 = dt8 * a_neg
        xs_bf = xc.astype(BF16)
        bt = bc.T
        st_old = st_ref[...]
        want_y = z_rows is not None
        if want_y:
            cc = _conv_silu(uc_ref, r0, cwc_ref, cbc_ref)
            cb = lax.dot_general(cc.astype(BF16), bc.astype(BF16), (((1,), (1,)), ((), ())),
                                 preferred_element_type=F32)
            st_bf = st_old.astype(BF16)
        lhs_y, rhs_y, lhs_s, rhs_s, tot = [], [], [], [], []
        for r in range(SSM_HPG):
            a_r = a8[r:r + 1, :]
            dt_r = dt8[r:r + 1, :]
            a_hi = a_r.astype(BF16)
            a_lo = (a_r - a_hi.astype(F32)).astype(BF16)
            m1 = m1_ref[...]
            dd = jnp.dot(jnp.concatenate([m1 * a_hi, m1 * a_lo], axis=1), m2_ref[...],
                         preferred_element_type=F32)
            dseg = dd[:, :CHUNK]
            acsb = dd[:, CHUNK:]
            lmat = jnp.exp(dseg) * tril_ref[...]
            xs_r = xs_bf * hm_ref[r]
            if want_y:
                lhs_y.append((cb * lmat * dt_r).astype(BF16))
                lhs_y.append((cc * jnp.exp(acsb)).astype(BF16))
                rhs_y.append(xs_r)
                rhs_y.append(st_bf * hm_ref[r])
            f1 = lmat[CHUNK - 1:CHUNK, :] * dt_r
            lhs_s.append((bt * f1).astype(BF16))
            rhs_s.append(xs_r)
            tot.append(acsb[CHUNK - 1:CHUNK, :])
        s_new = jnp.dot(jnp.concatenate(lhs_s, axis=1), jnp.concatenate(rhs_s, axis=0),
                        preferred_element_type=F32)
        half = lax.broadcasted_iota(jnp.int32, (1, CHUNK), 1) < SSM_HEAD_DIM
        decay = jnp.exp(jnp.concatenate([jnp.where(half, tot[0], tot[1]),
                                         jnp.where(half, tot[2], tot[3])], axis=1))
        st_ref[...] = st_old * decay + s_new
        if want_y:
            y = jnp.dot(jnp.concatenate(lhs_y, axis=1), jnp.concatenate(rhs_y, axis=0),
                        preferred_element_type=F32)
            y = y + xc * dskip_ref[...]
            gated = y * _silu(z_rows.astype(F32))
            ms = jnp.mean(gated * gated, axis=-1, keepdims=True)
            y_ref[0, pl.ds(out_row0, CHUNK), :] = (gated * lax.rsqrt(ms + RMS_EPS) * nw_ref[...]).astype(y_ref.dtype)

    lane_pos = lax.broadcasted_iota(jnp.int32, (1, CHUNK), 1)
    dt_meta = jnp.where(lane_pos >= PAD, _softplus(dtm_ref[0] + dt_bias), 0.0)
    chunk_step(U_LEAD, dt_meta, None, None)

    def body(s, carry):
        row0 = pl.multiple_of(s * CHUNK, CHUNK)
        dt8 = _softplus(dt_ref[0, :, pl.ds(row0, CHUNK)] + dt_bias)
        chunk_step(U_SEQ0 + row0, dt8, z_ref[0, pl.ds(row0, CHUNK), :], row0)
        return carry

    lax.fori_loop(0, n_chunks, body, 0, unroll=4)


def _ssd(proj3, proj_meta, dt3, dt3_meta, conv_w, conv_b2, dtb3, alog3, dskip_row, nw_row):
    nb, seq, _ = proj3.shape
    gx, gb, gc = COL_X // GROUP_W, COL_B // D_STATE, COL_C // D_STATE
    u_rows = U_SEQ0 + seq
    return pl.pallas_call(
        _ssd_kernel,
        grid=(nb, SSM_GROUPS),
        in_specs=[
            pl.BlockSpec((1, seq, GROUP_W), lambda b, g: (b, 0, gx + g)),
            pl.BlockSpec((1, seq, D_STATE), lambda b, g: (b, 0, gb + g)),
            pl.BlockSpec((1, seq, D_STATE), lambda b, g: (b, 0, gc + g)),
            pl.BlockSpec((1, seq, GROUP_W), lambda b, g: (b, 0, g)),
            pl.BlockSpec((CHUNK, GROUP_W), lambda b, g: (0, gx + g)),
            pl.BlockSpec((CHUNK, D_STATE), lambda b, g: (0, gb + g)),
            pl.BlockSpec((CHUNK, D_STATE), lambda b, g: (0, gc + g)),
            pl.BlockSpec((1, DT_ROWS, seq), lambda b, g: (g, 0, b)),
            pl.BlockSpec((1, DT_ROWS, CHUNK), lambda b, g: (g, 0, 0)),
            pl.BlockSpec((CONV_K, GROUP_W), lambda b, g: (0, g)),
            pl.BlockSpec((CONV_K, D_STATE), lambda b, g: (0, D_SSM // D_STATE + g)),
            pl.BlockSpec((CONV_K, D_STATE), lambda b, g: (0, D_SSM // D_STATE + SSM_GROUPS + g)),
            pl.BlockSpec((1, GROUP_W), lambda b, g: (0, g)),
            pl.BlockSpec((1, D_STATE), lambda b, g: (0, D_SSM // D_STATE + g)),
            pl.BlockSpec((1, D_STATE), lambda b, g: (0, D_SSM // D_STATE + SSM_GROUPS + g)),
            pl.BlockSpec((1, DT_ROWS, 1), lambda b, g: (g, 0, 0)),
            pl.BlockSpec((1, DT_ROWS, 1), lambda b, g: (g, 0, 0)),
            pl.BlockSpec((1, GROUP_W), lambda b, g: (0, g)),
            pl.BlockSpec((1, GROUP_W), lambda b, g: (0, g)),
        ],
        out_specs=pl.BlockSpec((1, seq, GROUP_W), lambda b, g: (b, 0, g)),
        out_shape=jax.ShapeDtypeStruct((nb, seq, D_SSM), BF16),
        scratch_shapes=[
            pltpu.VMEM((u_rows, GROUP_W), F32),
            pltpu.VMEM((u_rows, D_STATE), F32),
            pltpu.VMEM((u_rows, D_STATE), F32),
            pltpu.VMEM((D_STATE, GROUP_W), F32),
            pltpu.VMEM((CHUNK, CHUNK), BF16),
            pltpu.VMEM((CHUNK, CHUNK), F32),
            pltpu.VMEM((2 * CHUNK, 2 * CHUNK), BF16),
            pltpu.VMEM((SSM_HPG, CHUNK, GROUP_W), BF16),
        ],
        compiler_params=pltpu.CompilerParams(
            dimension_semantics=("arbitrary", "arbitrary"), vmem_limit_bytes=VMEM_LIMIT),
        name="ssd",
    )(proj3, proj3, proj3, proj3, proj_meta, proj_meta, proj_meta, dt3, dt3_meta,
      conv_w, conv_w, conv_w, conv_b2, conv_b2, conv_b2, dtb3, alog3, dskip_row, nw_row)


def _attn_kernel(slope_ref, q_ref, k_ref, v_ref, km_ref, vm_ref, lam_ref, nw_ref, o_ref,
                 kaug_ref, vt_ref, s_ref, p_ref):
    h = pl.program_id(1)
    seq = q_ref.shape[1]
    tq = ATT_TQ
    slope = slope_ref[h]

    def stage(r0, k, v, valid_from):
        nrows = k.shape[0]
        pos = lax.broadcasted_iota(jnp.int32, (nrows, 128), 0) + r0
        bias = slope * pos.astype(F32)
        if valid_from:
            bias = jnp.where(pos >= valid_from, bias, NEG_INF)
        b1 = bias.astype(BF16).astype(F32)
        b2 = (bias - b1).astype(BF16).astype(F32)
        b3 = bias - b1 - b2
        lane = lax.broadcasted_iota(jnp.int32, (nrows, 128), 1)
        cols = jnp.where(lane == 0, b1, jnp.where(lane == 1, b2, jnp.where(lane == 2, b3, 0.0)))
        kaug_ref[r0:r0 + nrows, 0:128] = k
        kaug_ref[r0:r0 + nrows, 128:256] = cols.astype(BF16)
        vt_ref[:, r0:r0 + nrows] = v.astype(F32).T.astype(BF16)

    stage(0, km_ref[...], vm_ref[...], PAD)
    for j in range(seq // CHUNK):
        rows = slice(CHUNK * j, CHUNK * (j + 1))
        stage(CHUNK * (j + 1), k_ref[0, rows, :], v_ref[0, rows, :], 0)

    lam4 = lam_ref[...]
    lam = (jnp.exp(jnp.sum(lam4[0:1] * lam4[1:2], axis=-1, keepdims=True))
           - jnp.exp(jnp.sum(lam4[2:3] * lam4[3:4], axis=-1, keepdims=True)) + LAMBDA_INIT)

    lane_q = lax.broadcasted_iota(jnp.int32, (tq, 128), 1)
    ones_blk = jnp.where(lane_q < 3, 1.0, 0.0).astype(BF16)
    map_lanes = (lane_q < ATT_QK_DIM, lane_q >= ATT_QK_DIM)
    causal = (lax.broadcasted_iota(jnp.int32, (tq, tq), 0)
              <= lax.broadcasted_iota(jnp.int32, (tq, tq), 1))

    for t in range(seq // tq):
        kend = CHUNK + tq * (t + 1)
        blocks = [(0, CHUNK)] + [(CHUNK + tq * j, CHUNK + tq * (j + 1)) for j in range(t + 1)]
        q = q_ref[0, tq * t:tq * (t + 1), :] * jnp.asarray(ATT_QK_DIM ** -0.5, BF16)
        outs = []
        for c in range(2):
            qaug = jnp.concatenate([jnp.where(map_lanes[c], q, jnp.zeros_like(q)), ones_blk], axis=1)
            m = None
            for r0, r1 in blocks:
                sv = lax.dot_general(kaug_ref[r0:r1, :], qaug, (((1,), (1,)), ((), ())),
                                     preferred_element_type=F32)
                if r1 == kend:
                    sv = jnp.where(causal, sv, NEG_INF)
                s_ref[c, r0:r1, :] = sv
                bm = jnp.max(sv, axis=0, keepdims=True)
                m = bm if m is None else jnp.maximum(m, bm)
            l = None
            for r0, r1 in blocks:
                p = jnp.exp(s_ref[c, r0:r1, :] - m)
                bl = jnp.sum(p, axis=0, keepdims=True)
                l = bl if l is None else l + bl
                p_ref[c, r0:r1, :] = p.astype(BF16)
            acc = jnp.dot(vt_ref[:, 0:kend], p_ref[c, 0:kend, :], preferred_element_type=F32)
            outs.append(acc / l)
        o = outs[0] - lam * outs[1]
        ms = jnp.mean(o * o, axis=0, keepdims=True)
        o = o * lax.rsqrt(ms + RMS_EPS) * nw_ref[...] * (1.0 - LAMBDA_INIT)
        o_ref[0, tq * t:tq * (t + 1), :] = o.T.astype(o_ref.dtype)


def _attention(slopes, proj3, proj_meta, lam4, nw_col):
    nb, seq, _ = proj3.shape
    cq, ck, cv = COL_Q // ATT_V_DIM, COL_K // ATT_V_DIM, COL_V // ATT_V_DIM
    kv_rows = CHUNK + seq
    seq_blk = lambda c0: pl.BlockSpec((1, seq, ATT_V_DIM), lambda b, h: (b, 0, c0 + h))
    meta_blk = lambda c0: pl.BlockSpec((CHUNK, ATT_V_DIM), lambda b, h: (0, c0 + h))
    return pl.pallas_call(
        _attn_kernel,
        grid=(nb, ATT_HEADS),
        in_specs=[
            pl.BlockSpec(memory_space=pltpu.SMEM),
            seq_blk(cq), seq_blk(ck), seq_blk(cv), meta_blk(ck), meta_blk(cv),
            pl.BlockSpec((4, ATT_QK_DIM), lambda b, h: (0, 0)),
            pl.BlockSpec((ATT_V_DIM, 1), lambda b, h: (0, 0)),
        ],
        out_specs=pl.BlockSpec((1, seq, ATT_V_DIM), lambda b, h: (b, 0, h)),
        out_shape=jax.ShapeDtypeStruct((nb, seq, D_ATT), BF16),
        scratch_shapes=[
            pltpu.VMEM((kv_rows, 2 * ATT_V_DIM), BF16),
            pltpu.VMEM((ATT_V_DIM, kv_rows), BF16),
            pltpu.VMEM((2, kv_rows, ATT_TQ), F32),
            pltpu.VMEM((2, kv_rows, ATT_TQ), BF16),
        ],
        compiler_params=pltpu.CompilerParams(
            dimension_semantics=("arbitrary", "arbitrary"), vmem_limit_bytes=VMEM_LIMIT),
        name="diffattn",
    )(slopes, proj3, proj3, proj3, proj_meta, proj_meta, lam4, nw_col)


def _outproj_kernel(y_ref, o_ref, wy_ref, wo_ref, x_ref, g0_ref, b0_ref, g1_ref, b1_ref,
                    h_ref, acc_ref, *, ln_rows):
    j = pl.program_id(1)
    tn = wy_ref.shape[1]
    c0 = pl.multiple_of(j * tn, tn)
    acc_ref[:, pl.ds(c0, tn)] = (jnp.dot(y_ref[...], wy_ref[...], preferred_element_type=F32)
                                 + jnp.dot(o_ref[...], wo_ref[...], preferred_element_type=F32))

    @pl.when(j == pl.num_programs(1) - 1)
    def _():
        tm = x_ref.shape[0]

        def body(r, carry):
            r0 = pl.multiple_of(r * ln_rows, ln_rows)
            rows = pl.ds(r0, ln_rows)
            h0 = _layer_norm_rows(x_ref[rows, :], g0_ref[...], b0_ref[...])
            h_ref[rows, :] = _layer_norm_rows(ALPHA * h0 + acc_ref[rows, :], g1_ref[...], b1_ref[...])
            return carry

        lax.fori_loop(0, tm // ln_rows, body, 0)


def _outproj(y2d, o2d, w_out, x2d, g0, b0, g1, b1, *, tm, tn):
    m = x2d.shape[0]
    row = lambda i, j: (0, 0)
    return pl.pallas_call(
        functools.partial(_outproj_kernel, ln_rows=128),
        grid=(m // tm, D_MODEL // tn),
        in_specs=[
            pl.BlockSpec((tm, D_SSM), lambda i, j: (i, 0)),
            pl.BlockSpec((tm, D_ATT), lambda i, j: (i, 0)),
            pl.BlockSpec((D_SSM, tn), lambda i, j: (0, j)),
            pl.BlockSpec((D_ATT, tn), lambda i, j: (1, j)),
            pl.BlockSpec((tm, D_MODEL), lambda i, j: (i, 0)),
            pl.BlockSpec((1, D_MODEL), row),
            pl.BlockSpec((1, D_MODEL), row),
            pl.BlockSpec((1, D_MODEL), row),
            pl.BlockSpec((1, D_MODEL), row),
        ],
        out_specs=pl.BlockSpec((tm, D_MODEL), lambda i, j: (i, 0)),
        out_shape=jax.ShapeDtypeStruct((m, D_MODEL), F32),
        scratch_shapes=[pltpu.VMEM((tm, D_MODEL), F32)],
        compiler_params=pltpu.CompilerParams(
            dimension_semantics=("arbitrary", "arbitrary"), vmem_limit_bytes=VMEM_LIMIT),
        name="outproj",
    )(y2d, o2d, w_out, w_out, x2d, g0, b0, g1, b1)


def _mlp_kernel(h_ref, wu_ref, wd_ref, g_ref, b_ref, o_ref, hb_ref, acc_ref, *, ln_rows):
    f = pl.program_id(1)

    @pl.when(f == 0)
    def _():
        hb_ref[...] = h_ref[...].astype(BF16)
        acc_ref[...] = jnp.zeros(acc_ref.shape, F32)

    u = jnp.dot(hb_ref[...], wu_ref[...], preferred_element_type=F32)
    u = jnp.square(jnp.maximum(u, 0.0)).astype(BF16)
    for c0 in range(0, D_MODEL, MLP_ACC_SLAB):
        cols = slice(c0, c0 + MLP_ACC_SLAB)
        acc_ref[:, cols] += jnp.dot(u, wd_ref[:, cols], preferred_element_type=F32)

    @pl.when(f == pl.num_programs(1) - 1)
    def _():
        tm = h_ref.shape[0]

        def body(r, carry):
            r0 = pl.multiple_of(r * ln_rows, ln_rows)
            rows = pl.ds(r0, ln_rows)
            o_ref[rows, :] = _layer_norm_rows(ALPHA * h_ref[rows, :] + acc_ref[rows, :], g_ref[...], b_ref[...])
            return carry

        lax.fori_loop(0, tm // ln_rows, body, 0)


def _mlp(h1, w_up, w_down, g, b, *, tm, tf):
    m = h1.shape[0]
    row = lambda i, f: (0, 0)
    return pl.pallas_call(
        functools.partial(_mlp_kernel, ln_rows=128),
        grid=(m // tm, D_FF // tf),
        in_specs=[
            pl.BlockSpec((tm, D_MODEL), lambda i, f: (i, 0)),
            pl.BlockSpec((D_MODEL, tf), lambda i, f: (0, f)),
            pl.BlockSpec((tf, D_MODEL), lambda i, f: (f, 0)),
            pl.BlockSpec((1, D_MODEL), row),
            pl.BlockSpec((1, D_MODEL), row),
        ],
        out_specs=pl.BlockSpec((tm, D_MODEL), lambda i, f: (i, 0)),
        out_shape=jax.ShapeDtypeStruct((m, D_MODEL), F32),
        scratch_shapes=[pltpu.VMEM((tm, D_MODEL), BF16), pltpu.VMEM((tm, D_MODEL), F32)],
        compiler_params=pltpu.CompilerParams(
            dimension_semantics=("arbitrary", "arbitrary"), vmem_limit_bytes=VMEM_LIMIT),
        name="mlp",
    )(h1, w_up, w_down, g, b)


def kernel(x, meta_tokens, ln0_g, ln0_b, w_in, conv_w, conv_b, dt_bias, a_log, d_skip, ssd_norm_w,
           lambda_q1, lambda_k1, lambda_q2, lambda_k2, attn_norm_w, w_out, ln1_g, ln1_b, w_up, w_down,
           ln2_g, ln2_b):
    nb, seq, d = x.shape
    assert (d, w_in.shape[0]) == (D_MODEL, DEPTH) and seq % ATT_TQ == 0
    m = nb * seq
    x2d = x.reshape(m, d)
    row = lambda v: v.reshape(1, -1).astype(F32)

    w_in_t = w_in[0].astype(F32).T
    wdt = w_in_t[W_IN_DT:W_IN_DT + SSM_HEADS].reshape(SSM_GROUPS, SSM_HPG, d)
    wdt = jnp.pad(wdt, ((0, 0), (0, DT_ROWS - SSM_HPG), (0, 0))).reshape(SSM_GROUPS * DT_ROWS, d).astype(BF16)
    pad_heads = lambda v: jnp.pad(v.reshape(SSM_GROUPS, SSM_HPG).astype(F32),
                                  ((0, 0), (0, DT_ROWS - SSM_HPG)))[..., None]
    dtb3, alog3 = pad_heads(dt_bias[0]), pad_heads(a_log[0])
    dskip_row = row(jnp.repeat(d_skip[0], SSM_HEAD_DIM))
    lam4 = jnp.stack([lambda_q1[0], lambda_k1[0], lambda_q2[0], lambda_k2[0]]).astype(F32)
    slopes = jnp.asarray(2.0 ** (-8.0 * np.arange(1, ATT_HEADS + 1) / ATT_HEADS), dtype=F32)
    w_out_b, w_up_b, w_down_b = w_out[0].astype(BF16), w_up[0].astype(BF16), w_down[0].astype(BF16)
    g0, b0 = row(ln0_g), row(ln0_b)
    meta_pad = jnp.pad(meta_tokens.astype(F32), ((PAD, 0), (0, 0)))

    proj, dt_t, proj_meta, dt_t_meta = _inproj(
        x2d, meta_pad, g0, b0, w_in_t.reshape(-1, SSM_HEADS, d), wdt, tm=1024, tn=1024)
    proj3 = proj.reshape(nb, seq, N_PROJ)
    dt3 = dt_t.reshape(SSM_GROUPS, DT_ROWS, m)
    dt3_meta = dt_t_meta.reshape(SSM_GROUPS, DT_ROWS, CHUNK)

    y = _ssd(proj3, proj_meta, dt3, dt3_meta, conv_w[0].astype(F32), row(conv_b[0]),
             dtb3, alog3, dskip_row, row(ssd_norm_w[0]))
    o = _attention(slopes, proj3, proj_meta, lam4, attn_norm_w[0].reshape(ATT_V_DIM, 1).astype(F32))

    h1 = _outproj(y.reshape(m, D_SSM), o.reshape(m, D_ATT), w_out_b, x2d, g0, b0,
                  row(ln1_g[0]), row(ln1_b[0]), tm=512, tn=512)
    h2 = _mlp(h1, w_up_b, w_down_b, row(ln2_g[0]), row(ln2_b[0]), tm=512, tf=1024)
    return h2.reshape(nb, seq, d)
```

```python
import functools
import math

import jax
import jax.numpy as jnp
import numpy as np
from jax import lax
from jax.experimental import pallas as pl
from jax.experimental.pallas import tpu as pltpu

F32 = jnp.float32
BF16 = jnp.bfloat16

D_MODEL = 2048
N_META = 16
CHUNK = 128
PAD = CHUNK - N_META
D_SSM = 2048
D_ATT = 2048
SSM_HEAD_DIM = 64
SSM_HEADS = 32
SSM_GROUPS = 8
SSM_HPG = 4
GROUP_W = SSM_HPG * SSM_HEAD_DIM
D_STATE = 128
CONV_K = 4
D_CONV = D_SSM + 2 * SSM_GROUPS * D_STATE
ATT_V_DIM = 128
ATT_HEADS = 16
ATT_QK_DIM = 64
D_FF = 4 * D_MODEL
DEPTH = 1
ALPHA = (2 * DEPTH) ** 0.25
LN_EPS = 1e-5
RMS_EPS = 1e-5
NEG_INF = -1e30
LOG2E = math.log2(math.e)
LAMBDA_INIT = 0.8 - 0.6 * math.exp(-0.3 * 0)

N_PROJ = D_SSM + D_CONV + 3 * D_ATT
COL_Z = 0
COL_X = D_SSM
COL_B = COL_X + D_SSM
COL_C = COL_B + SSM_GROUPS * D_STATE
COL_Q = COL_C + SSM_GROUPS * D_STATE
COL_K = COL_Q + D_ATT
COL_V = COL_K + D_ATT
W_IN_DT = D_SSM + D_CONV
DT_ROWS = 8

V7X_VMEM_BYTES = 64 * 1024 * 1024
VMEM_LIMIT = 56 * 1024 * 1024

ATT_TQ = 512
MLP_ACC_SLAB = 512


def _layer_norm_rows(x, g, b):
    mu = jnp.mean(x, axis=-1, keepdims=True)
    xc = x - mu
    var = jnp.mean(xc * xc, axis=-1, keepdims=True)
    return xc * lax.rsqrt(var + LN_EPS) * g + b


def _silu(x):
    hx = 0.5 * x
    return hx + hx * jnp.tanh(hx)


def _softplus(x):
    return jnp.maximum(x, 0.0) + jnp.log1p(jnp.exp(-jnp.abs(x)))


CONTRACT_LAST = (((1,), (1,)), ((), ()))


def _ln0_kernel(x_ref, xm_ref, g_ref, b_ref, wdt_ref, hn_ref, dt_ref, hm_ref, dtm_ref, *, ln_rows):
    def body(r, carry):
        r0 = pl.multiple_of(r * ln_rows, ln_rows)
        hn = _layer_norm_rows(x_ref[pl.ds(r0, ln_rows), :], g_ref[...], b_ref[...])
        hn_ref[pl.ds(r0, ln_rows), :] = hn.astype(BF16)
        return carry

    lax.fori_loop(0, x_ref.shape[0] // ln_rows, body, 0)
    dt_ref[...] = lax.dot_general(wdt_ref[...], hn_ref[...], CONTRACT_LAST, preferred_element_type=F32)

    @pl.when(pl.program_id(0) == 0)
    def _():
        hm_ref[...] = _layer_norm_rows(xm_ref[...], g_ref[...], b_ref[...]).astype(BF16)
        dtm_ref[...] = lax.dot_general(wdt_ref[...], hm_ref[...], CONTRACT_LAST, preferred_element_type=F32)


def _ln0(x2d, meta_pad, g, b, wdt, *, tm):
    m = x2d.shape[0]
    ndt = wdt.shape[0]
    const = lambda i: (0, 0)
    return pl.pallas_call(
        functools.partial(_ln0_kernel, ln_rows=128),
        grid=(m // tm,),
        in_specs=[
            pl.BlockSpec((tm, D_MODEL), lambda i: (i, 0)),
            pl.BlockSpec((CHUNK, D_MODEL), const),
            pl.BlockSpec((1, D_MODEL), const),
            pl.BlockSpec((1, D_MODEL), const),
            pl.BlockSpec((ndt, D_MODEL), const),
        ],
        out_specs=[
            pl.BlockSpec((tm, D_MODEL), lambda i: (i, 0)),
            pl.BlockSpec((ndt, tm), lambda i: (0, i)),
            pl.BlockSpec((CHUNK, D_MODEL), const),
            pl.BlockSpec((ndt, CHUNK), const),
        ],
        out_shape=[
            jax.ShapeDtypeStruct((m, D_MODEL), BF16),
            jax.ShapeDtypeStruct((ndt, m), F32),
            jax.ShapeDtypeStruct((CHUNK, D_MODEL), BF16),
            jax.ShapeDtypeStruct((ndt, CHUNK), F32),
        ],
        compiler_params=pltpu.CompilerParams(dimension_semantics=("arbitrary",), vmem_limit_bytes=VMEM_LIMIT),
        name="ln0",
    )(x2d, meta_pad, g, b, wdt)


def _inproj_kernel(hn_ref, hm_ref, wt_ref, o_ref, om_ref, wb_ref):
    @pl.when(pl.program_id(1) == 0)
    def _():
        wb_ref[...] = wt_ref[...].reshape(wb_ref.shape).astype(BF16)
        om_ref[...] = lax.dot_general(hm_ref[...], wb_ref[...], CONTRACT_LAST,
                                      preferred_element_type=F32).astype(om_ref.dtype)

    o_ref[...] = lax.dot_general(hn_ref[...], wb_ref[...], CONTRACT_LAST,
                                 preferred_element_type=F32).astype(o_ref.dtype)


def _inproj(hn, hm, wt, *, tm, tn):
    m = hn.shape[0]
    n = N_PROJ
    assert W_IN_DT % tn == 0 and n % tn == 0 and tn % SSM_HEADS == 0 and wt.shape[1] == SSM_HEADS
    groups = tn // SSM_HEADS
    group_start = lambda j: j * groups + (j >= W_IN_DT // tn).astype(jnp.int32)
    return pl.pallas_call(
        _inproj_kernel,
        grid=(n // tn, m // tm),
        in_specs=[
            pl.BlockSpec((tm, D_MODEL), lambda j, i: (i, 0)),
            pl.BlockSpec((CHUNK, D_MODEL), lambda j, i: (0, 0)),
            pl.BlockSpec((pl.Element(groups), pl.Element(SSM_HEADS), pl.Element(D_MODEL)),
                         lambda j, i: (group_start(j), 0, 0)),
        ],
        out_specs=[
            pl.BlockSpec((tm, tn), lambda j, i: (i, j)),
            pl.BlockSpec((CHUNK, tn), lambda j, i: (0, j)),
        ],
        out_shape=[
            jax.ShapeDtypeStruct((m, n), BF16),
            jax.ShapeDtypeStruct((CHUNK, n), BF16),
        ],
        scratch_shapes=[pltpu.VMEM((tn, D_MODEL), BF16)],
        compiler_params=pltpu.CompilerParams(
            dimension_semantics=("arbitrary", "arbitrary"), vmem_limit_bytes=VMEM_LIMIT),
        name="inproj",
    )(hn, hm, wt)


U_LEAD = 16
U_SEQ0 = U_LEAD + CHUNK


def _conv_silu(u_ref, r0, w_ref, b_ref):
    win = u_ref[pl.ds(r0 - 8, CHUNK + 8), :]
    w = w_ref[...]
    acc = b_ref[...] + w[3:4, :] * win[8:CHUNK + 8, :]
    acc = acc + w[2:3, :] * win[7:CHUNK + 7, :]
    acc = acc + w[1:2, :] * win[6:CHUNK + 6, :]
    acc = acc + w[0:1, :] * win[5:CHUNK + 5, :]
    return _silu(acc)


def _ssd_kernel(xs_ref, b_ref, c_ref, z_ref, xm_ref, bm_ref, cm_ref, dt_ref, dtm_ref,
                cwx_ref, cwb_ref, cwc_ref, cbx_ref, cbb_ref, cbc_ref,
                dtb_ref, alog_ref, dskip_ref, nw_ref,
                y_ref, ux_ref, ub_ref, uc_ref, st_ref, m1_ref, tril_ref, m2_ref, hm_ref):
    seq = xs_ref.shape[1]
    n_chunks = seq // CHUNK

    li = lax.broadcasted_iota(jnp.int32, (CHUNK, CHUNK), 0)
    ki = lax.broadcasted_iota(jnp.int32, (CHUNK, CHUNK), 1)
    lower = jnp.where(ki <= li, 1.0, 0.0)
    tril_ref[...] = lower
    m1_ref[...] = lower.astype(BF16)
    k2 = lax.broadcasted_iota(jnp.int32, (2 * CHUNK, 2 * CHUNK), 0) & (CHUNK - 1)
    s2 = lax.broadcasted_iota(jnp.int32, (2 * CHUNK, 2 * CHUNK), 1)
    m2_ref[...] = jnp.where((s2 >= CHUNK) | (k2 > s2), 1.0, 0.0).astype(BF16)
    head_of = lax.broadcasted_iota(jnp.int32, (CHUNK, GROUP_W), 1) >> 6
    for r in range(SSM_HPG):
        hm_ref[r] = jnp.where(head_of == r, 1.0, 0.0).astype(BF16)

    row = lax.broadcasted_iota(jnp.int32, (CHUNK, 1), 0)
    meta_valid = row >= PAD
    for u_ref, m_ref, s_ref in ((ux_ref, xm_ref, xs_ref), (ub_ref, bm_ref, b_ref), (uc_ref, cm_ref, c_ref)):
        ncol = u_ref.shape[1]
        u_ref[0:U_LEAD, :] = jnp.zeros((U_LEAD, ncol), F32)
        u_ref[U_LEAD:U_SEQ0, :] = jnp.where(meta_valid, m_ref[...].astype(F32), 0.0)
        u_ref[U_SEQ0:U_SEQ0 + seq, :] = s_ref[0].astype(F32)
    st_ref[...] = jnp.zeros(st_ref.shape, F32)

    dt_bias = dtb_ref[0]
    a_neg = -jnp.exp(alog_ref[0])

    def chunk_step(r0, dt8, z_rows, out_row0):
        xc = _conv_silu(ux_ref, r0, cwx_ref, cbx_ref)
        bc = _conv_silu(ub_ref, r0, cwb_ref, cbb_ref)
        a8 = dt8 * a_neg
        xs_bf = xc.astype(BF16)
        bt = bc.T
        st_old = st_ref[...]
        want_y = z_rows is not None
        if want_y:
            cc = _conv_silu(uc_ref, r0, cwc_ref, cbc_ref)
            cb = lax.dot_general(cc.astype(BF16), bc.astype(BF16), CONTRACT_LAST,
                                 preferred_element_type=F32)
            st_bf = st_old.astype(BF16)
        lhs_y, rhs_y, lhs_s, rhs_s, tot = [], [], [], [], []
        for r in range(SSM_HPG):
            a_r = a8[r:r + 1, :]
            dt_r = dt8[r:r + 1, :]
            a_hi = a_r.astype(BF16)
            a_lo = (a_r - a_hi.astype(F32)).astype(BF16)
            m1 = m1_ref[...]
            dd = jnp.dot(jnp.concatenate([m1 * a_hi, m1 * a_lo], axis=1), m2_ref[...],
                         preferred_element_type=F32)
            dseg = dd[:, :CHUNK]
            acsb = dd[:, CHUNK:]
            lmat = jnp.exp(dseg) * tril_ref[...]
            xs_r = xs_bf * hm_ref[r]
            if want_y:
                lhs_y.append((cb * lmat * dt_r).astype(BF16))
                lhs_y.append((cc * jnp.exp(acsb)).astype(BF16))
                rhs_y.append(xs_r)
                rhs_y.append(st_bf * hm_ref[r])
            f1 = lmat[CHUNK - 1:CHUNK, :] * dt_r
            lhs_s.append((bt * f1).astype(BF16))
            rhs_s.append(xs_r)
            tot.append(acsb[CHUNK - 1:CHUNK, :])
        s_new = jnp.dot(jnp.concatenate(lhs_s, axis=1), jnp.concatenate(rhs_s, axis=0),
                        preferred_element_type=F32)
        half = lax.broadcasted_iota(jnp.int32, (1, CHUNK), 1) < SSM_HEAD_DIM
        decay = jnp.exp(jnp.concatenate([jnp.where(half, tot[0], tot[1]),
                                         jnp.where(half, tot[2], tot[3])], axis=1))
        st_ref[...] = st_old * decay + s_new
        if want_y:
            y = jnp.dot(jnp.concatenate(lhs_y, axis=1), jnp.concatenate(rhs_y, axis=0),
                        preferred_element_type=F32)
            y = y + xc * dskip_ref[...]
            gated = y * _silu(z_rows.astype(F32))
            ms = jnp.mean(gated * gated, axis=-1, keepdims=True)
            y_ref[0, pl.ds(out_row0, CHUNK), :] = (gated * lax.rsqrt(ms + RMS_EPS) * nw_ref[...]).astype(y_ref.dtype)

    lane_pos = lax.broadcasted_iota(jnp.int32, (1, CHUNK), 1)
    dt_meta = jnp.where(lane_pos >= PAD, _softplus(dtm_ref[0] + dt_bias), 0.0)
    chunk_step(U_LEAD, dt_meta, None, None)

    def body(s, carry):
        row0 = pl.multiple_of(s * CHUNK, CHUNK)
        dt8 = _softplus(dt_ref[0, :, pl.ds(row0, CHUNK)] + dt_bias)
        chunk_step(U_SEQ0 + row0, dt8, z_ref[0, pl.ds(row0, CHUNK), :], row0)
        return carry

    lax.fori_loop(0, n_chunks, body, 0, unroll=4)


def _ssd(proj3, proj_meta, dt3, dt3_meta, conv_w, conv_b2, dtb3, alog3, dskip_row, nw_row):
    nb, seq, _ = proj3.shape
    gx, gb, gc = COL_X // GROUP_W, COL_B // D_STATE, COL_C // D_STATE
    u_rows = U_SEQ0 + seq
    return pl.pallas_call(
        _ssd_kernel,
        grid=(nb, SSM_GROUPS),
        in_specs=[
            pl.BlockSpec((1, seq, GROUP_W), lambda b, g: (b, 0, gx + g)),
            pl.BlockSpec((1, seq, D_STATE), lambda b, g: (b, 0, gb + g)),
            pl.BlockSpec((1, seq, D_STATE), lambda b, g: (b, 0, gc + g)),
            pl.BlockSpec((1, seq, GROUP_W), lambda b, g: (b, 0, g)),
            pl.BlockSpec((CHUNK, GROUP_W), lambda b, g: (0, gx + g)),
            pl.BlockSpec((CHUNK, D_STATE), lambda b, g: (0, gb + g)),
            pl.BlockSpec((CHUNK, D_STATE), lambda b, g: (0, gc + g)),
            pl.BlockSpec((1, DT_ROWS, seq), lambda b, g: (g, 0, b)),
            pl.BlockSpec((1, DT_ROWS, CHUNK), lambda b, g: (g, 0, 0)),
            pl.BlockSpec((CONV_K, GROUP_W), lambda b, g: (0, g)),
            pl.BlockSpec((CONV_K, D_STATE), lambda b, g: (0, D_SSM // D_STATE + g)),
            pl.BlockSpec((CONV_K, D_STATE), lambda b, g: (0, D_SSM // D_STATE + SSM_GROUPS + g)),
            pl.BlockSpec((1, GROUP_W), lambda b, g: (0, g)),
            pl.BlockSpec((1, D_STATE), lambda b, g: (0, D_SSM // D_STATE + g)),
            pl.BlockSpec((1, D_STATE), lambda b, g: (0, D_SSM // D_STATE + SSM_GROUPS + g)),
            pl.BlockSpec((1, DT_ROWS, 1), lambda b, g: (g, 0, 0)),
            pl.BlockSpec((1, DT_ROWS, 1), lambda b, g: (g, 0, 0)),
            pl.BlockSpec((1, GROUP_W), lambda b, g: (0, g)),
            pl.BlockSpec((1, GROUP_W), lambda b, g: (0, g)),
        ],
        out_specs=pl.BlockSpec((1, seq, GROUP_W), lambda b, g: (b, 0, g)),
        out_shape=jax.ShapeDtypeStruct((nb, seq, D_SSM), BF16),
        scratch_shapes=[
            pltpu.VMEM((u_rows, GROUP_W), F32),
            pltpu.VMEM((u_rows, D_STATE), F32),
            pltpu.VMEM((u_rows, D_STATE), F32),
            pltpu.VMEM((D_STATE, GROUP_W), F32),
            pltpu.VMEM((CHUNK, CHUNK), BF16),
            pltpu.VMEM((CHUNK, CHUNK), F32),
            pltpu.VMEM((2 * CHUNK, 2 * CHUNK), BF16),
            pltpu.VMEM((SSM_HPG, CHUNK, GROUP_W), BF16),
        ],
        compiler_params=pltpu.CompilerParams(
            dimension_semantics=("arbitrary", "arbitrary"), vmem_limit_bytes=VMEM_LIMIT),
        name="ssd",
    )(proj3, proj3, proj3, proj3, proj_meta, proj_meta, proj_meta, dt3, dt3_meta,
      conv_w, conv_w, conv_w, conv_b2, conv_b2, conv_b2, dtb3, alog3, dskip_row, nw_row)


def _attn_kernel(slope_ref, q_ref, k_ref, v_ref, km_ref, vm_ref, lam_ref, nw_ref, wu_ref, wd_ref,
                 o_ref, wub_ref, wdb_ref, kaug_ref, vt_ref, s_ref, p_ref):
    wub_ref[...] = wu_ref[...].astype(BF16)
    wdb_ref[...] = wd_ref[...].astype(BF16)

    h = pl.program_id(1)
    seq = q_ref.shape[1]
    tq = ATT_TQ
    slope = slope_ref[h]

    def stage(r0, k, v, valid_from):
        nrows = k.shape[0]
        pos = lax.broadcasted_iota(jnp.int32, (nrows, 128), 0) + r0
        bias = (slope * LOG2E) * pos.astype(F32)
        if valid_from:
            bias = jnp.where(pos >= valid_from, bias, NEG_INF)
        b1 = bias.astype(BF16).astype(F32)
        b2 = (bias - b1).astype(BF16).astype(F32)
        b3 = bias - b1 - b2
        lane = lax.broadcasted_iota(jnp.int32, (nrows, 128), 1)
        cols = jnp.where(lane == 0, b1, jnp.where(lane == 1, b2, jnp.where(lane == 2, b3, 0.0)))
        kaug_ref[r0:r0 + nrows, 0:128] = (k.astype(F32) * LOG2E).astype(BF16)
        kaug_ref[r0:r0 + nrows, 128:256] = cols.astype(BF16)
        vt_ref[:, r0:r0 + nrows] = v.astype(F32).T.astype(BF16)

    stage(0, km_ref[...], vm_ref[...], PAD)
    for j in range(seq // CHUNK):
        rows = slice(CHUNK * j, CHUNK * (j + 1))
        stage(CHUNK * (j + 1), k_ref[0, rows, :], v_ref[0, rows, :], 0)

    lam4 = lam_ref[...]
    lam = (jnp.exp(jnp.sum(lam4[0:1] * lam4[1:2], axis=-1, keepdims=True))
           - jnp.exp(jnp.sum(lam4[2:3] * lam4[3:4], axis=-1, keepdims=True)) + LAMBDA_INIT)

    lane_q = lax.broadcasted_iota(jnp.int32, (tq, 128), 1)
    ones_blk = jnp.where(lane_q < 3, 1.0, 0.0).astype(BF16)
    map_lanes = (lane_q < ATT_QK_DIM, lane_q >= ATT_QK_DIM)
    causal = (lax.broadcasted_iota(jnp.int32, (tq, tq), 0)
              <= lax.broadcasted_iota(jnp.int32, (tq, tq), 1))

    for t in range(seq // tq):
        kend = CHUNK + tq * (t + 1)
        blocks = [(0, CHUNK)] + [(CHUNK + tq * j, CHUNK + tq * (j + 1)) for j in range(t + 1)]
        q = q_ref[0, tq * t:tq * (t + 1), :] * jnp.asarray(ATT_QK_DIM ** -0.5, BF16)
        outs = []
        for c in range(2):
            qaug = jnp.concatenate([jnp.where(map_lanes[c], q, jnp.zeros_like(q)), ones_blk], axis=1)
            m = None
            for r0, r1 in blocks:
                sv = lax.dot_general(kaug_ref[r0:r1, :], qaug, CONTRACT_LAST,
                                     preferred_element_type=F32)
                if r1 == kend:
                    sv = jnp.where(causal, sv, NEG_INF)
                s_ref[c, r0:r1, :] = sv
                bm = jnp.max(sv, axis=0, keepdims=True)
                m = bm if m is None else jnp.maximum(m, bm)
            l = None
            for r0, r1 in blocks:
                p = jnp.exp2(s_ref[c, r0:r1, :] - m)
                bl = jnp.sum(p, axis=0, keepdims=True)
                l = bl if l is None else l + bl
                p_ref[c, r0:r1, :] = p.astype(BF16)
            acc = jnp.dot(vt_ref[:, 0:kend], p_ref[c, 0:kend, :], preferred_element_type=F32)
            outs.append(acc / l)
        o = outs[0] - lam * outs[1]
        ms = jnp.mean(o * o, axis=0, keepdims=True)
        o = o * lax.rsqrt(ms + RMS_EPS) * nw_ref[...] * (1.0 - LAMBDA_INIT)
        o_ref[0, tq * t:tq * (t + 1), :] = o.T.astype(o_ref.dtype)


def _attention(slopes, proj3, proj_meta, lam4, nw_col, w_up, w_down):
    nb, seq, _ = proj3.shape
    steps = nb * ATT_HEADS
    wu3 = w_up.reshape(steps, D_MODEL // steps, D_FF)
    wd3 = w_down.reshape(steps, D_FF // steps, D_MODEL)
    w_blk = lambda a: pl.BlockSpec((1,) + a.shape[1:], lambda b, h: (b * ATT_HEADS + h, 0, 0))
    cq, ck, cv = COL_Q // ATT_V_DIM, COL_K // ATT_V_DIM, COL_V // ATT_V_DIM
    kv_rows = CHUNK + seq
    seq_blk = lambda c0: pl.BlockSpec((1, seq, ATT_V_DIM), lambda b, h: (b, 0, c0 + h))
    meta_blk = lambda c0: pl.BlockSpec((CHUNK, ATT_V_DIM), lambda b, h: (0, c0 + h))
    return pl.pallas_call(
        _attn_kernel,
        grid=(nb, ATT_HEADS),
        in_specs=[
            pl.BlockSpec(memory_space=pltpu.SMEM),
            seq_blk(cq), seq_blk(ck), seq_blk(cv), meta_blk(ck), meta_blk(cv),
            pl.BlockSpec((4, ATT_QK_DIM), lambda b, h: (0, 0)),
            pl.BlockSpec((ATT_V_DIM, 1), lambda b, h: (0, 0)),
            w_blk(wu3), w_blk(wd3),
        ],
        out_specs=[pl.BlockSpec((1, seq, ATT_V_DIM), lambda b, h: (b, 0, h)), w_blk(wu3), w_blk(wd3)],
        out_shape=[jax.ShapeDtypeStruct((nb, seq, D_ATT), BF16),
                   jax.ShapeDtypeStruct(wu3.shape, BF16), jax.ShapeDtypeStruct(wd3.shape, BF16)],
        scratch_shapes=[
            pltpu.VMEM((kv_rows, 2 * ATT_V_DIM), BF16),
            pltpu.VMEM((ATT_V_DIM, kv_rows), BF16),
            pltpu.VMEM((2, kv_rows, ATT_TQ), F32),
            pltpu.VMEM((2, kv_rows, ATT_TQ), BF16),
        ],
        compiler_params=pltpu.CompilerParams(
            dimension_semantics=("arbitrary", "arbitrary"), vmem_limit_bytes=VMEM_LIMIT),
        name="diffattn",
    )(slopes, proj3, proj3, proj3, proj_meta, proj_meta, lam4, nw_col, wu3, wd3)


def _outproj_kernel(y_ref, o_ref, wy_ref, wo_ref, mix_ref, wyb_ref, wob_ref):
    @pl.when(pl.program_id(1) == 0)
    def _():
        wyb_ref[...] = wy_ref[...].astype(BF16)
        wob_ref[...] = wo_ref[...].astype(BF16)

    mix_ref[...] = (jnp.dot(y_ref[...], wyb_ref[...], preferred_element_type=F32)
                    + jnp.dot(o_ref[...], wob_ref[...], preferred_element_type=F32))


def _outproj(y2d, o2d, w_out, *, tm, tn):
    m = y2d.shape[0]
    return pl.pallas_call(
        _outproj_kernel,
        grid=(D_MODEL // tn, m // tm),
        in_specs=[
            pl.BlockSpec((tm, D_SSM), lambda j, i: (i, 0)),
            pl.BlockSpec((tm, D_ATT), lambda j, i: (i, 0)),
            pl.BlockSpec((D_SSM, tn), lambda j, i: (0, j)),
            pl.BlockSpec((D_ATT, tn), lambda j, i: (1, j)),
        ],
        out_specs=pl.BlockSpec((tm, tn), lambda j, i: (i, j)),
        out_shape=jax.ShapeDtypeStruct((m, D_MODEL), F32),
        scratch_shapes=[pltpu.VMEM((D_SSM, tn), BF16), pltpu.VMEM((D_ATT, tn), BF16)],
        compiler_params=pltpu.CompilerParams(
            dimension_semantics=("arbitrary", "arbitrary"), vmem_limit_bytes=VMEM_LIMIT),
        name="outproj",
    )(y2d, o2d, w_out, w_out)


def _mlp_kernel(x_ref, mix_ref, wu_ref, wd_ref, g0_ref, b0_ref, g1_ref, b1_ref, g2_ref, b2_ref,
                o_ref, h1_ref, hb_ref, *, ln_rows):
    f = pl.program_id(1)
    n_ln = x_ref.shape[0] // ln_rows

    @pl.when(f == 0)
    def _():
        def body(r, carry):
            rows = pl.ds(pl.multiple_of(r * ln_rows, ln_rows), ln_rows)
            h0 = _layer_norm_rows(x_ref[rows, :], g0_ref[...], b0_ref[...])
            h1 = _layer_norm_rows(ALPHA * h0 + mix_ref[rows, :], g1_ref[...], b1_ref[...])
            h1_ref[rows, :] = h1
            hb_ref[rows, :] = h1.astype(BF16)
            return carry

        lax.fori_loop(0, n_ln, body, 0)
        o_ref[...] = jnp.zeros(o_ref.shape, F32)

    u = jnp.dot(hb_ref[...], wu_ref[...], preferred_element_type=F32)
    u = jnp.square(jnp.maximum(u, 0.0)).astype(BF16)
    for c0 in range(0, D_MODEL, MLP_ACC_SLAB):
        cols = slice(c0, c0 + MLP_ACC_SLAB)
        o_ref[:, cols] += jnp.dot(u, wd_ref[:, cols], preferred_element_type=F32)

    @pl.when(f == pl.num_programs(1) - 1)
    def _():
        def body(r, carry):
            rows = pl.ds(pl.multiple_of(r * ln_rows, ln_rows), ln_rows)
            o_ref[rows, :] = _layer_norm_rows(ALPHA * h1_ref[rows, :] + o_ref[rows, :], g2_ref[...], b2_ref[...])
            return carry

        lax.fori_loop(0, n_ln, body, 0)


def _mlp(x2d, mix, w_up, w_down, g0, b0, g1, b1, g2, b2, *, tm, tf):
    m = x2d.shape[0]
    row = pl.BlockSpec((1, D_MODEL), lambda i, f: (0, 0))
    return pl.pallas_call(
        functools.partial(_mlp_kernel, ln_rows=128),
        grid=(m // tm, D_FF // tf),
        in_specs=[
            pl.BlockSpec((tm, D_MODEL), lambda i, f: (i, 0)),
            pl.BlockSpec((tm, D_MODEL), lambda i, f: (i, 0)),
            pl.BlockSpec((D_MODEL, tf), lambda i, f: (0, f)),
            pl.BlockSpec((tf, D_MODEL), lambda i, f: (f, 0)),
            row, row, row, row, row, row,
        ],
        out_specs=pl.BlockSpec((tm, D_MODEL), lambda i, f: (i, 0)),
        out_shape=jax.ShapeDtypeStruct((m, D_MODEL), F32),
        scratch_shapes=[pltpu.VMEM((tm, D_MODEL), F32), pltpu.VMEM((tm, D_MODEL), BF16)],
        compiler_params=pltpu.CompilerParams(
            dimension_semantics=("arbitrary", "arbitrary"), vmem_limit_bytes=VMEM_LIMIT),
        name="mlp",
    )(x2d, mix, w_up, w_down, g0, b0, g1, b1, g2, b2)


def kernel(x, meta_tokens, ln0_g, ln0_b, w_in, conv_w, conv_b, dt_bias, a_log, d_skip, ssd_norm_w,
           lambda_q1, lambda_k1, lambda_q2, lambda_k2, attn_norm_w, w_out, ln1_g, ln1_b, w_up, w_down,
           ln2_g, ln2_b):
    nb, seq, d = x.shape
    assert (d, w_in.shape[0]) == (D_MODEL, DEPTH) and seq % ATT_TQ == 0
    m = nb * seq
    x2d = x.reshape(m, d)
    row = lambda v: v.reshape(1, -1).astype(F32)

    w_in_t = w_in[0].astype(F32).T
    wdt = w_in_t[W_IN_DT:W_IN_DT + SSM_HEADS].reshape(SSM_GROUPS, SSM_HPG, d)
    wdt = jnp.pad(wdt, ((0, 0), (0, DT_ROWS - SSM_HPG), (0, 0))).reshape(SSM_GROUPS * DT_ROWS, d).astype(BF16)
    pad_heads = lambda v: jnp.pad(v.reshape(SSM_GROUPS, SSM_HPG).astype(F32),
                                  ((0, 0), (0, DT_ROWS - SSM_HPG)))[..., None]
    dtb3, alog3 = pad_heads(dt_bias[0]), pad_heads(a_log[0])
    dskip_row = row(jnp.repeat(d_skip[0], SSM_HEAD_DIM))
    lam4 = jnp.stack([lambda_q1[0], lambda_k1[0], lambda_q2[0], lambda_k2[0]]).astype(F32)
    slopes = jnp.asarray(2.0 ** (-8.0 * np.arange(1, ATT_HEADS + 1) / ATT_HEADS), dtype=F32)
    g0, b0 = row(ln0_g), row(ln0_b)
    meta_pad = jnp.pad(meta_tokens.astype(F32), ((PAD, 0), (0, 0)))

    hn, dt_t, hm, dt_t_meta = _ln0(x2d, meta_pad, g0, b0, wdt, tm=1024)
    proj, proj_meta = _inproj(hn, hm, w_in_t.reshape(-1, SSM_HEADS, d), tm=2048, tn=768)
    proj3 = proj.reshape(nb, seq, N_PROJ)
    dt3 = dt_t.reshape(SSM_GROUPS, DT_ROWS, m)
    dt3_meta = dt_t_meta.reshape(SSM_GROUPS, DT_ROWS, CHUNK)

    y = _ssd(proj3, proj_meta, dt3, dt3_meta, conv_w[0].astype(F32), row(conv_b[0]),
             dtb3, alog3, dskip_row, row(ssd_norm_w[0]))
    o, w_up_b, w_down_b = _attention(slopes, proj3, proj_meta, lam4,
                                     attn_norm_w[0].reshape(ATT_V_DIM, 1).astype(F32),
                                     w_up[0].astype(F32), w_down[0].astype(F32))

    mix = _outproj(y.reshape(m, D_SSM), o.reshape(m, D_ATT), w_out[0].astype(F32), tm=1024, tn=512)
    h2 = _mlp(x2d, mix, w_up_b.reshape(D_MODEL, D_FF), w_down_b.reshape(D_FF, D_MODEL),
              g0, b0, row(ln1_g[0]), row(ln1_b[0]), row(ln2_g[0]), row(ln2_b[0]), tm=512, tf=1024)
    return h2.reshape(nb, seq, d)
```

```python
import functools
import math

import jax
import jax.numpy as jnp
import numpy as np
from jax import lax
from jax.experimental import pallas as pl
from jax.experimental.pallas import tpu as pltpu

F32 = jnp.float32
BF16 = jnp.bfloat16

D_MODEL = 2048
N_META = 16
CHUNK = 128
PAD = CHUNK - N_META
D_SSM = 2048
D_ATT = 2048
SSM_HEAD_DIM = 64
SSM_HEADS = 32
SSM_GROUPS = 8
SSM_HPG = 4
GROUP_W = SSM_HPG * SSM_HEAD_DIM
D_STATE = 128
CONV_K = 4
D_CONV = D_SSM + 2 * SSM_GROUPS * D_STATE
ATT_V_DIM = 128
ATT_HEADS = 16
ATT_QK_DIM = 64
D_FF = 4 * D_MODEL
DEPTH = 1
ALPHA = (2 * DEPTH) ** 0.25
LN_EPS = 1e-5
RMS_EPS = 1e-5
NEG_INF = -1e30
LOG2E = math.log2(math.e)
LAMBDA_INIT = 0.8 - 0.6 * math.exp(-0.3 * 0)

N_PROJ = D_SSM + D_CONV + 3 * D_ATT
COL_Z = 0
COL_X = D_SSM
COL_B = COL_X + D_SSM
COL_C = COL_B + SSM_GROUPS * D_STATE
COL_Q = COL_C + SSM_GROUPS * D_STATE
COL_K = COL_Q + D_ATT
COL_V = COL_K + D_ATT
W_IN_DT = D_SSM + D_CONV
DT_ROWS = 8

V7X_VMEM_BYTES = 64 * 1024 * 1024
VMEM_LIMIT = 56 * 1024 * 1024

ATT_TQ = 512
MLP_ACC_SLAB = 512


def _layer_norm_rows(x, g, b):
    mu = jnp.mean(x, axis=-1, keepdims=True)
    xc = x - mu
    var = jnp.mean(xc * xc, axis=-1, keepdims=True)
    return xc * lax.rsqrt(var + LN_EPS) * g + b


def _silu(x):
    hx = 0.5 * x
    return hx + hx * jnp.tanh(hx)


def _softplus(x):
    return jnp.maximum(x, 0.0) + jnp.log1p(jnp.exp(-jnp.abs(x)))


CONTRACT_LAST = (((1,), (1,)), ((), ()))


def _ln0_kernel(x_ref, xm_ref, g_ref, b_ref, wdt_ref, hn_ref, dt_ref, hm_ref, dtm_ref, *, ln_rows):
    def body(r, carry):
        r0 = pl.multiple_of(r * ln_rows, ln_rows)
        hn = _layer_norm_rows(x_ref[pl.ds(r0, ln_rows), :], g_ref[...], b_ref[...])
        hn_ref[pl.ds(r0, ln_rows), :] = hn.astype(BF16)
        return carry

    lax.fori_loop(0, x_ref.shape[0] // ln_rows, body, 0)
    dt_ref[...] = lax.dot_general(wdt_ref[...], hn_ref[...], CONTRACT_LAST, preferred_element_type=F32)

    @pl.when(pl.program_id(0) == 0)
    def _():
        hm_ref[...] = _layer_norm_rows(xm_ref[...], g_ref[...], b_ref[...]).astype(BF16)
        dtm_ref[...] = lax.dot_general(wdt_ref[...], hm_ref[...], CONTRACT_LAST, preferred_element_type=F32)


def _ln0(x2d, meta_pad, g, b, wdt, *, tm):
    m = x2d.shape[0]
    ndt = wdt.shape[0]
    const = lambda i: (0, 0)
    return pl.pallas_call(
        functools.partial(_ln0_kernel, ln_rows=128),
        grid=(m // tm,),
        in_specs=[
            pl.BlockSpec((tm, D_MODEL), lambda i: (i, 0)),
            pl.BlockSpec((CHUNK, D_MODEL), const),
            pl.BlockSpec((1, D_MODEL), const),
            pl.BlockSpec((1, D_MODEL), const),
            pl.BlockSpec((ndt, D_MODEL), const),
        ],
        out_specs=[
            pl.BlockSpec((tm, D_MODEL), lambda i: (i, 0)),
            pl.BlockSpec((ndt, tm), lambda i: (0, i)),
            pl.BlockSpec((CHUNK, D_MODEL), const),
            pl.BlockSpec((ndt, CHUNK), const),
        ],
        out_shape=[
            jax.ShapeDtypeStruct((m, D_MODEL), BF16),
            jax.ShapeDtypeStruct((ndt, m), F32),
            jax.ShapeDtypeStruct((CHUNK, D_MODEL), BF16),
            jax.ShapeDtypeStruct((ndt, CHUNK), F32),
        ],
        compiler_params=pltpu.CompilerParams(dimension_semantics=("arbitrary",), vmem_limit_bytes=VMEM_LIMIT),
        name="ln0",
    )(x2d, meta_pad, g, b, wdt)


def _inproj_kernel(hn_ref, hm_ref, wt_ref, o_ref, om_ref, wb_ref):
    @pl.when(pl.program_id(1) == 0)
    def _():
        wb_ref[...] = wt_ref[...].reshape(wb_ref.shape).astype(BF16)
        om_ref[...] = lax.dot_general(hm_ref[...], wb_ref[...], CONTRACT_LAST,
                                      preferred_element_type=F32).astype(om_ref.dtype)

    o_ref[...] = lax.dot_general(hn_ref[...], wb_ref[...], CONTRACT_LAST,
                                 preferred_element_type=F32).astype(o_ref.dtype)


def _inproj(hn, hm, wt, *, tm, tn):
    m = hn.shape[0]
    n = N_PROJ
    assert W_IN_DT % tn == 0 and n % tn == 0 and tn % SSM_HEADS == 0 and wt.shape[1] == SSM_HEADS
    groups = tn // SSM_HEADS
    group_start = lambda j: j * groups + (j >= W_IN_DT // tn).astype(jnp.int32)
    return pl.pallas_call(
        _inproj_kernel,
        grid=(n // tn, m // tm),
        in_specs=[
            pl.BlockSpec((tm, D_MODEL), lambda j, i: (i, 0)),
            pl.BlockSpec((CHUNK, D_MODEL), lambda j, i: (0, 0)),
            pl.BlockSpec((pl.Element(groups), pl.Element(SSM_HEADS), pl.Element(D_MODEL)),
                         lambda j, i: (group_start(j), 0, 0)),
        ],
        out_specs=[
            pl.BlockSpec((tm, tn), lambda j, i: (i, j)),
            pl.BlockSpec((CHUNK, tn), lambda j, i: (0, j)),
        ],
        out_shape=[
            jax.ShapeDtypeStruct((m, n), BF16),
            jax.ShapeDtypeStruct((CHUNK, n), BF16),
        ],
        scratch_shapes=[pltpu.VMEM((tn, D_MODEL), BF16)],
        compiler_params=pltpu.CompilerParams(
            dimension_semantics=("arbitrary", "arbitrary"), vmem_limit_bytes=VMEM_LIMIT),
        name="inproj",
    )(hn, hm, wt)


U_LEAD = 16
U_SEQ0 = U_LEAD + CHUNK


def _conv_silu(u_ref, r0, w_ref, b_ref):
    win = u_ref[pl.ds(r0 - 8, CHUNK + 8), :]
    w = w_ref[...]
    acc = b_ref[...] + w[3:4, :] * win[8:CHUNK + 8, :]
    acc = acc + w[2:3, :] * win[7:CHUNK + 7, :]
    acc = acc + w[1:2, :] * win[6:CHUNK + 6, :]
    acc = acc + w[0:1, :] * win[5:CHUNK + 5, :]
    return _silu(acc)


def _ssd_kernel(xs_ref, b_ref, c_ref, z_ref, xm_ref, bm_ref, cm_ref, dt_ref, dtm_ref,
                cwx_ref, cwb_ref, cwc_ref, cbx_ref, cbb_ref, cbc_ref,
                dtb_ref, alog_ref, dskip_ref, nw_ref,
                y_ref, ux_ref, ub_ref, uc_ref, st_ref, m1_ref, tril_ref, m2_ref, hm_ref):
    seq = xs_ref.shape[1]
    n_chunks = seq // CHUNK

    li = lax.broadcasted_iota(jnp.int32, (CHUNK, CHUNK), 0)
    ki = lax.broadcasted_iota(jnp.int32, (CHUNK, CHUNK), 1)
    lower = jnp.where(ki <= li, 1.0, 0.0)
    tril_ref[...] = lower
    m1_ref[...] = lower.astype(BF16)
    k2 = lax.broadcasted_iota(jnp.int32, (2 * CHUNK, 2 * CHUNK), 0) & (CHUNK - 1)
    s2 = lax.broadcasted_iota(jnp.int32, (2 * CHUNK, 2 * CHUNK), 1)
    m2_ref[...] = jnp.where((s2 >= CHUNK) | (k2 > s2), 1.0, 0.0).astype(BF16)
    head_of = lax.broadcasted_iota(jnp.int32, (CHUNK, GROUP_W), 1) >> 6
    for r in range(SSM_HPG):
        hm_ref[r] = jnp.where(head_of == r, 1.0, 0.0).astype(BF16)

    row = lax.broadcasted_iota(jnp.int32, (CHUNK, 1), 0)
    meta_valid = row >= PAD
    for u_ref, m_ref, s_ref in ((ux_ref, xm_ref, xs_ref), (ub_ref, bm_ref, b_ref), (uc_ref, cm_ref, c_ref)):
        ncol = u_ref.shape[1]
        u_ref[0:U_LEAD, :] = jnp.zeros((U_LEAD, ncol), F32)
        u_ref[U_LEAD:U_SEQ0, :] = jnp.where(meta_valid, m_ref[...].astype(F32), 0.0)
        u_ref[U_SEQ0:U_SEQ0 + seq, :] = s_ref[0].astype(F32)
    st_ref[...] = jnp.zeros(st_ref.shape, F32)

    dt_bias = dtb_ref[0]
    a_neg = -jnp.exp(alog_ref[0])

    def chunk_step(r0, dt8, z_rows, out_row0):
        xc = _conv_silu(ux_ref, r0, cwx_ref, cbx_ref)
        bc = _conv_silu(ub_ref, r0, cwb_ref, cbb_ref)
        a8 = dt8 * a_neg
        xs_bf = xc.astype(BF16)
        bt = bc.T
        st_old = st_ref[...]
        want_y = z_rows is not None
        if want_y:
            cc = _conv_silu(uc_ref, r0, cwc_ref, cbc_ref)
            cb = lax.dot_general(cc.astype(BF16), bc.astype(BF16), CONTRACT_LAST,
                                 preferred_element_type=F32)
            st_bf = st_old.astype(BF16)
        lhs_y, rhs_y, lhs_s, rhs_s, tot = [], [], [], [], []
        for r in range(SSM_HPG):
            a_r = a8[r:r + 1, :]
            dt_r = dt8[r:r + 1, :]
            a_hi = a_r.astype(BF16)
            a_lo = (a_r - a_hi.astype(F32)).astype(BF16)
            m1 = m1_ref[...]
            dd = jnp.dot(jnp.concatenate([m1 * a_hi, m1 * a_lo], axis=1), m2_ref[...],
                         preferred_element_type=F32)
            dseg = dd[:, :CHUNK]
            acsb = dd[:, CHUNK:]
            lmat = jnp.exp(dseg) * tril_ref[...]
            xs_r = xs_bf * hm_ref[r]
            if want_y:
                lhs_y.append((cb * lmat * dt_r).astype(BF16))
                lhs_y.append((cc * jnp.exp(acsb)).astype(BF16))
                rhs_y.append(xs_r)
                rhs_y.append(st_bf * hm_ref[r])
            f1 = lmat[CHUNK - 1:CHUNK, :] * dt_r
            lhs_s.append((bt * f1).astype(BF16))
            rhs_s.append(xs_r)
            tot.append(acsb[CHUNK - 1:CHUNK, :])
        s_new = jnp.dot(jnp.concatenate(lhs_s, axis=1), jnp.concatenate(rhs_s, axis=0),
                        preferred_element_type=F32)
        half = lax.broadcasted_iota(jnp.int32, (1, CHUNK), 1) < SSM_HEAD_DIM
        decay = jnp.exp(jnp.concatenate([jnp.where(half, tot[0], tot[1]),
                                         jnp.where(half, tot[2], tot[3])], axis=1))
        st_ref[...] = st_old * decay + s_new
        if want_y:
            y = jnp.dot(jnp.concatenate(lhs_y, axis=1), jnp.concatenate(rhs_y, axis=0),
                        preferred_element_type=F32)
            y = y + xc * dskip_ref[...]
            gated = y * _silu(z_rows.astype(F32))
            ms = jnp.mean(gated * gated, axis=-1, keepdims=True)
            y_ref[0, pl.ds(out_row0, CHUNK), :] = (gated * lax.rsqrt(ms + RMS_EPS) * nw_ref[...]).astype(y_ref.dtype)

    lane_pos = lax.broadcasted_iota(jnp.int32, (1, CHUNK), 1)
    dt_meta = jnp.where(lane_pos >= PAD, _softplus(dtm_ref[0] + dt_bias), 0.0)
    chunk_step(U_LEAD, dt_meta, None, None)

    def body(s, carry):
        row0 = pl.multiple_of(s * CHUNK, CHUNK)
        dt8 = _softplus(dt_ref[0, :, pl.ds(row0, CHUNK)] + dt_bias)
        chunk_step(U_SEQ0 + row0, dt8, z_ref[0, pl.ds(row0, CHUNK), :], row0)
        return carry

    lax.fori_loop(0, n_chunks, body, 0, unroll=4)


def _ssd(proj3, proj_meta, dt3, dt3_meta, conv_w, conv_b2, dtb3, alog3, dskip_row, nw_row):
    nb, seq, _ = proj3.shape
    gx, gb, gc = COL_X // GROUP_W, COL_B // D_STATE, COL_C // D_STATE
    u_rows = U_SEQ0 + seq
    return pl.pallas_call(
        _ssd_kernel,
        grid=(nb, SSM_GROUPS),
        in_specs=[
            pl.BlockSpec((1, seq, GROUP_W), lambda b, g: (b, 0, gx + g)),
            pl.BlockSpec((1, seq, D_STATE), lambda b, g: (b, 0, gb + g)),
            pl.BlockSpec((1, seq, D_STATE), lambda b, g: (b, 0, gc + g)),
            pl.BlockSpec((1, seq, GROUP_W), lambda b, g: (b, 0, g)),
            pl.BlockSpec((CHUNK, GROUP_W), lambda b, g: (0, gx + g)),
            pl.BlockSpec((CHUNK, D_STATE), lambda b, g: (0, gb + g)),
            pl.BlockSpec((CHUNK, D_STATE), lambda b, g: (0, gc + g)),
            pl.BlockSpec((1, DT_ROWS, seq), lambda b, g: (g, 0, b)),
            pl.BlockSpec((1, DT_ROWS, CHUNK), lambda b, g: (g, 0, 0)),
            pl.BlockSpec((CONV_K, GROUP_W), lambda b, g: (0, g)),
            pl.BlockSpec((CONV_K, D_STATE), lambda b, g: (0, D_SSM // D_STATE + g)),
            pl.BlockSpec((CONV_K, D_STATE), lambda b, g: (0, D_SSM // D_STATE + SSM_GROUPS + g)),
            pl.BlockSpec((1, GROUP_W), lambda b, g: (0, g)),
            pl.BlockSpec((1, D_STATE), lambda b, g: (0, D_SSM // D_STATE + g)),
            pl.BlockSpec((1, D_STATE), lambda b, g: (0, D_SSM // D_STATE + SSM_GROUPS + g)),
            pl.BlockSpec((1, DT_ROWS, 1), lambda b, g: (g, 0, 0)),
            pl.BlockSpec((1, DT_ROWS, 1), lambda b, g: (g, 0, 0)),
            pl.BlockSpec((1, GROUP_W), lambda b, g: (0, g)),
            pl.BlockSpec((1, GROUP_W), lambda b, g: (0, g)),
        ],
        out_specs=pl.BlockSpec((1, seq, GROUP_W), lambda b, g: (b, 0, g)),
        out_shape=jax.ShapeDtypeStruct((nb, seq, D_SSM), BF16),
        scratch_shapes=[
            pltpu.VMEM((u_rows, GROUP_W), F32),
            pltpu.VMEM((u_rows, D_STATE), F32),
            pltpu.VMEM((u_rows, D_STATE), F32),
            pltpu.VMEM((D_STATE, GROUP_W), F32),
            pltpu.VMEM((CHUNK, CHUNK), BF16),
            pltpu.VMEM((CHUNK, CHUNK), F32),
            pltpu.VMEM((2 * CHUNK, 2 * CHUNK), BF16),
            pltpu.VMEM((SSM_HPG, CHUNK, GROUP_W), BF16),
        ],
        compiler_params=pltpu.CompilerParams(
            dimension_semantics=("arbitrary", "arbitrary"), vmem_limit_bytes=VMEM_LIMIT),
        name="ssd",
    )(proj3, proj3, proj3, proj3, proj_meta, proj_meta, proj_meta, dt3, dt3_meta,
      conv_w, conv_w, conv_w, conv_b2, conv_b2, conv_b2, dtb3, alog3, dskip_row, nw_row)


def _attn_kernel(slope_ref, q_ref, k_ref, v_ref, km_ref, vm_ref, lam_ref, nw_ref, wu_ref, wd_ref,
                 o_ref, wub_ref, wdb_ref, kaug_ref, vt_ref, s_ref, p_ref):
    wub_ref[...] = wu_ref[...].astype(BF16)
    wdb_ref[...] = wd_ref[...].astype(BF16)

    h = pl.program_id(1)
    seq = q_ref.shape[1]
    tq = ATT_TQ
    slope = slope_ref[h]

    sub8 = lax.broadcasted_iota(jnp.int32, (8, CHUNK), 0)
    lane8 = lax.broadcasted_iota(jnp.int32, (8, CHUNK), 1)

    def stage(r0, k, v, valid_from):
        pos = lane8 + r0
        bias = (slope * LOG2E) * pos.astype(F32)
        if valid_from:
            bias = jnp.where(pos >= valid_from, bias, NEG_INF)
        b1 = bias.astype(BF16).astype(F32)
        b2 = (bias - b1).astype(BF16).astype(F32)
        b3 = bias - b1 - b2
        rows8 = jnp.where(sub8 == 0, b1, jnp.where(sub8 == 1, b2, jnp.where(sub8 == 2, b3, 0.0)))
        cols = jnp.concatenate([rows8, jnp.zeros((CHUNK - 8, CHUNK), F32)], axis=0).T
        kaug_ref[r0:r0 + CHUNK, 0:128] = (k.astype(F32) * LOG2E).astype(BF16)
        kaug_ref[r0:r0 + CHUNK, 128:256] = cols.astype(BF16)
        vt_ref[0:ATT_V_DIM, r0:r0 + CHUNK] = v.T

    vt_ref[ATT_V_DIM:, :] = jnp.ones((vt_ref.shape[0] - ATT_V_DIM, vt_ref.shape[1]), BF16)
    stage(0, km_ref[...], vm_ref[...], PAD)
    for j in range(seq // CHUNK):
        rows = slice(CHUNK * j, CHUNK * (j + 1))
        stage(CHUNK * (j + 1), k_ref[0, rows, :], v_ref[0, rows, :], 0)

    lam4 = lam_ref[...]
    lam = (jnp.exp(jnp.sum(lam4[0:1] * lam4[1:2], axis=-1, keepdims=True))
           - jnp.exp(jnp.sum(lam4[2:3] * lam4[3:4], axis=-1, keepdims=True)) + LAMBDA_INIT)

    lane_q = lax.broadcasted_iota(jnp.int32, (tq, 128), 1)
    ones_blk = jnp.where(lane_q < 3, 1.0, 0.0).astype(BF16)
    map_lanes = (lane_q < ATT_QK_DIM, lane_q >= ATT_QK_DIM)
    hq = tq // 2
    tri = (lax.broadcasted_iota(jnp.int32, (hq, hq), 0)
           <= lax.broadcasted_iota(jnp.int32, (hq, hq), 1))

    def scores(rows, qa):
        return lax.dot_general(kaug_ref[rows, :], qa, CONTRACT_LAST, preferred_element_type=F32)

    class Item:
        def __init__(self, t, c):
            self.t, self.c = t, c
            d0 = CHUNK + tq * t
            self.kend = d0 + tq
            self.rows_a, self.rows_b = slice(d0, d0 + hq), slice(d0 + hq, self.kend)
            self.blocks = [slice(0, CHUNK)] + [slice(CHUNK + tq * j, CHUNK + tq * (j + 1)) for j in range(t)]
            self.m = None
            self.qaug = None

        def score_steps(self):
            c = self.c

            def first():
                q = q_ref[0, tq * self.t:tq * (self.t + 1), :] * jnp.asarray(ATT_QK_DIM ** -0.5, BF16)
                self.qaug = jnp.concatenate([jnp.where(map_lanes[c], q, jnp.zeros_like(q)), ones_blk], axis=1)

            def full(rows):
                def step():
                    sv = scores(rows, self.qaug)
                    s_ref[c, rows, :] = sv
                    bm = jnp.max(sv, axis=0, keepdims=True)
                    self.m = bm if self.m is None else jnp.maximum(self.m, bm)
                return step

            def diag_a():
                sv = scores(self.rows_a, self.qaug)
                sv = jnp.concatenate([jnp.where(tri, sv[:, :hq], NEG_INF), sv[:, hq:]], axis=1)
                s_ref[c, self.rows_a, :] = sv
                self.m = jnp.maximum(self.m, jnp.max(sv, axis=0, keepdims=True))

            def diag_b():
                sv = jnp.where(tri, scores(self.rows_b, self.qaug[hq:, :]), NEG_INF)
                s_ref[c, self.rows_b, hq:] = sv
                m = self.m
                self.m = jnp.concatenate(
                    [m[:, :hq], jnp.maximum(m[:, hq:], jnp.max(sv, axis=0, keepdims=True))], axis=1)

            return [first] + [full(r) for r in self.blocks] + [diag_a, diag_b]

        def prob_steps(self):
            c = self.c

            def full(rows):
                def step():
                    p_ref[c, rows, :] = jnp.exp2(s_ref[c, rows, :] - self.m).astype(BF16)
                return step

            def diag_b():
                p_ref[c, self.rows_b, :hq] = jnp.zeros((hq, hq), BF16)
                p_ref[c, self.rows_b, hq:] = jnp.exp2(s_ref[c, self.rows_b, hq:] - self.m[:, hq:]).astype(BF16)

            return [full(r) for r in self.blocks + [self.rows_a]] + [diag_b]

        def value_step(self):
            acc = jnp.dot(vt_ref[:, 0:self.kend], p_ref[self.c, 0:self.kend, :],
                          preferred_element_type=F32)
            return acc[:ATT_V_DIM] / acc[ATT_V_DIM:ATT_V_DIM + 1]

    items = [Item(t, c) for t in range(seq // tq) for c in range(2)]
    n_items = len(items)
    outs = {}

    def finish(i):
        item = items[i]
        outs[i] = item.value_step()
        if item.c == 1:
            o = outs.pop(i - 1) - lam * outs.pop(i)
            ms = jnp.mean(o * o, axis=0, keepdims=True)
            o = o * lax.rsqrt(ms + RMS_EPS) * nw_ref[...] * (1.0 - LAMBDA_INIT)
            o_ref[0, tq * item.t:tq * (item.t + 1), :] = o.T.astype(o_ref.dtype)

    for stage in range(-2, n_items):
        lanes = []
        if 0 <= stage + 2 < n_items:
            lanes.append(items[stage + 2].score_steps())
        if 0 <= stage + 1 < n_items:
            lanes.append(items[stage + 1].prob_steps())
        if 0 <= stage:
            lanes.append([functools.partial(finish, stage)])
        for k in range(max(len(steps) for steps in lanes)):
            for steps in lanes:
                if k < len(steps):
                    steps[k]()


def _attention(slopes, proj3, proj_meta, lam4, nw_col, w_up, w_down):
    nb, seq, _ = proj3.shape
    steps = nb * ATT_HEADS
    wu3 = w_up.reshape(steps, D_MODEL // steps, D_FF)
    wd3 = w_down.reshape(steps, D_FF // steps, D_MODEL)
    w_blk = lambda a: pl.BlockSpec((1,) + a.shape[1:], lambda b, h: (b * ATT_HEADS + h, 0, 0))
    cq, ck, cv = COL_Q // ATT_V_DIM, COL_K // ATT_V_DIM, COL_V // ATT_V_DIM
    kv_rows = CHUNK + seq
    seq_blk = lambda c0: pl.BlockSpec((1, seq, ATT_V_DIM), lambda b, h: (b, 0, c0 + h))
    meta_blk = lambda c0: pl.BlockSpec((CHUNK, ATT_V_DIM), lambda b, h: (0, c0 + h))
    return pl.pallas_call(
        _attn_kernel,
        grid=(nb, ATT_HEADS),
        in_specs=[
            pl.BlockSpec(memory_space=pltpu.SMEM),
            seq_blk(cq), seq_blk(ck), seq_blk(cv), meta_blk(ck), meta_blk(cv),
            pl.BlockSpec((4, ATT_QK_DIM), lambda b, h: (0, 0)),
            pl.BlockSpec((ATT_V_DIM, 1), lambda b, h: (0, 0)),
            w_blk(wu3), w_blk(wd3),
        ],
        out_specs=[pl.BlockSpec((1, seq, ATT_V_DIM), lambda b, h: (b, 0, h)), w_blk(wu3), w_blk(wd3)],
        out_shape=[jax.ShapeDtypeStruct((nb, seq, D_ATT), BF16),
                   jax.ShapeDtypeStruct(wu3.shape, BF16), jax.ShapeDtypeStruct(wd3.shape, BF16)],
        scratch_shapes=[
            pltpu.VMEM((kv_rows, 2 * ATT_V_DIM), BF16),
            pltpu.VMEM((ATT_V_DIM + 16, kv_rows), BF16),
            pltpu.VMEM((2, kv_rows, ATT_TQ), F32),
            pltpu.VMEM((2, kv_rows, ATT_TQ), BF16),
        ],
        compiler_params=pltpu.CompilerParams(
            dimension_semantics=("arbitrary", "arbitrary"), vmem_limit_bytes=VMEM_LIMIT),
        name="diffattn",
    )(slopes, proj3, proj3, proj3, proj_meta, proj_meta, lam4, nw_col, wu3, wd3)


def _outproj_kernel(y_ref, o_ref, wy_ref, wo_ref, mix_ref, wyb_ref, wob_ref):
    @pl.when(pl.program_id(1) == 0)
    def _():
        wyb_ref[...] = wy_ref[...].astype(BF16)
        wob_ref[...] = wo_ref[...].astype(BF16)

    mix_ref[...] = (jnp.dot(y_ref[...], wyb_ref[...], preferred_element_type=F32)
                    + jnp.dot(o_ref[...], wob_ref[...], preferred_element_type=F32))


def _outproj(y2d, o2d, w_out, *, tm, tn):
    m = y2d.shape[0]
    return pl.pallas_call(
        _outproj_kernel,
        grid=(D_MODEL // tn, m // tm),
        in_specs=[
            pl.BlockSpec((tm, D_SSM), lambda j, i: (i, 0)),
            pl.BlockSpec((tm, D_ATT), lambda j, i: (i, 0)),
            pl.BlockSpec((D_SSM, tn), lambda j, i: (0, j)),
            pl.BlockSpec((D_ATT, tn), lambda j, i: (1, j)),
        ],
        out_specs=pl.BlockSpec((tm, tn), lambda j, i: (i, j)),
        out_shape=jax.ShapeDtypeStruct((m, D_MODEL), F32),
        scratch_shapes=[pltpu.VMEM((D_SSM, tn), BF16), pltpu.VMEM((D_ATT, tn), BF16)],
        compiler_params=pltpu.CompilerParams(
            dimension_semantics=("arbitrary", "arbitrary"), vmem_limit_bytes=VMEM_LIMIT),
        name="outproj",
    )(y2d, o2d, w_out, w_out)


def _mlp_kernel(x_ref, mix_ref, wu_ref, wd_ref, g0_ref, b0_ref, g1_ref, b1_ref, g2_ref, b2_ref,
                o_ref, h1_ref, hb_ref, *, ln_rows):
    f = pl.program_id(1)
    n_ln = x_ref.shape[0] // ln_rows

    @pl.when(f == 0)
    def _():
        def body(r, carry):
            rows = pl.ds(pl.multiple_of(r * ln_rows, ln_rows), ln_rows)
            h0 = _layer_norm_rows(x_ref[rows, :], g0_ref[...], b0_ref[...])
            h1 = _layer_norm_rows(ALPHA * h0 + mix_ref[rows, :], g1_ref[...], b1_ref[...])
            h1_ref[rows, :] = h1
            hb_ref[rows, :] = h1.astype(BF16)
            return carry

        lax.fori_loop(0, n_ln, body, 0)
        o_ref[...] = jnp.zeros(o_ref.shape, F32)

    u = jnp.dot(hb_ref[...], wu_ref[...], preferred_element_type=F32)
    u = jnp.square(jnp.maximum(u, 0.0)).astype(BF16)
    for c0 in range(0, D_MODEL, MLP_ACC_SLAB):
        cols = slice(c0, c0 + MLP_ACC_SLAB)
        o_ref[:, cols] += jnp.dot(u, wd_ref[:, cols], preferred_element_type=F32)

    @pl.when(f == pl.num_programs(1) - 1)
    def _():
        def body(r, carry):
            rows = pl.ds(pl.multiple_of(r * ln_rows, ln_rows), ln_rows)
            o_ref[rows, :] = _layer_norm_rows(ALPHA * h1_ref[rows, :] + o_ref[rows, :], g2_ref[...], b2_ref[...])
            return carry

        lax.fori_loop(0, n_ln, body, 0)


def _mlp(x2d, mix, w_up, w_down, g0, b0, g1, b1, g2, b2, *, tm, tf):
    m = x2d.shape[0]
    row = pl.BlockSpec((1, D_MODEL), lambda i, f: (0, 0))
    return pl.pallas_call(
        functools.partial(_mlp_kernel, ln_rows=128),
        grid=(m // tm, D_FF // tf),
        in_specs=[
            pl.BlockSpec((tm, D_MODEL), lambda i, f: (i, 0)),
            pl.BlockSpec((tm, D_MODEL), lambda i, f: (i, 0)),
            pl.BlockSpec((D_MODEL, tf), lambda i, f: (0, f)),
            pl.BlockSpec((tf, D_MODEL), lambda i, f: (f, 0)),
            row, row, row, row, row, row,
        ],
        out_specs=pl.BlockSpec((tm, D_MODEL), lambda i, f: (i, 0)),
        out_shape=jax.ShapeDtypeStruct((m, D_MODEL), F32),
        scratch_shapes=[pltpu.VMEM((tm, D_MODEL), F32), pltpu.VMEM((tm, D_MODEL), BF16)],
        compiler_params=pltpu.CompilerParams(
            dimension_semantics=("arbitrary", "arbitrary"), vmem_limit_bytes=VMEM_LIMIT),
        name="mlp",
    )(x2d, mix, w_up, w_down, g0, b0, g1, b1, g2, b2)


def kernel(x, meta_tokens, ln0_g, ln0_b, w_in, conv_w, conv_b, dt_bias, a_log, d_skip, ssd_norm_w,
           lambda_q1, lambda_k1, lambda_q2, lambda_k2, attn_norm_w, w_out, ln1_g, ln1_b, w_up, w_down,
           ln2_g, ln2_b):
    nb, seq, d = x.shape
    assert (d, w_in.shape[0]) == (D_MODEL, DEPTH) and seq % ATT_TQ == 0
    m = nb * seq
    x2d = x.reshape(m, d)
    row = lambda v: v.reshape(1, -1).astype(F32)

    w_in_t = w_in[0].astype(F32).T
    wdt = w_in_t[W_IN_DT:W_IN_DT + SSM_HEADS].reshape(SSM_GROUPS, SSM_HPG, d)
    wdt = jnp.pad(wdt, ((0, 0), (0, DT_ROWS - SSM_HPG), (0, 0))).reshape(SSM_GROUPS * DT_ROWS, d).astype(BF16)
    pad_heads = lambda v: jnp.pad(v.reshape(SSM_GROUPS, SSM_HPG).astype(F32),
                                  ((0, 0), (0, DT_ROWS - SSM_HPG)))[..., None]
    dtb3, alog3 = pad_heads(dt_bias[0]), pad_heads(a_log[0])
    dskip_row = row(jnp.repeat(d_skip[0], SSM_HEAD_DIM))
    lam4 = jnp.stack([lambda_q1[0], lambda_k1[0], lambda_q2[0], lambda_k2[0]]).astype(F32)
    slopes = jnp.asarray(2.0 ** (-8.0 * np.arange(1, ATT_HEADS + 1) / ATT_HEADS), dtype=F32)
    g0, b0 = row(ln0_g), row(ln0_b)
    meta_pad = jnp.pad(meta_tokens.astype(F32), ((PAD, 0), (0, 0)))

    hn, dt_t, hm, dt_t_meta = _ln0(x2d, meta_pad, g0, b0, wdt, tm=1024)
    proj, proj_meta = _inproj(hn, hm, w_in_t.reshape(-1, SSM_HEADS, d), tm=2048, tn=768)
    proj3 = proj.reshape(nb, seq, N_PROJ)
    dt3 = dt_t.reshape(SSM_GROUPS, DT_ROWS, m)
    dt3_meta = dt_t_meta.reshape(SSM_GROUPS, DT_ROWS, CHUNK)

    y = _ssd(proj3, proj_meta, dt3, dt3_meta, conv_w[0].astype(F32), row(conv_b[0]),
             dtb3, alog3, dskip_row, row(ssd_norm_w[0]))
    o, w_up_b, w_down_b = _attention(slopes, proj3, proj_meta, lam4,
                                     attn_norm_w[0].reshape(ATT_V_DIM, 1).astype(F32),
                                     w_up[0].astype(F32), w_down[0].astype(F32))

    mix = _outproj(y.reshape(m, D_SSM), o.reshape(m, D_ATT), w_out[0].astype(F32), tm=1024, tn=512)
    h2 = _mlp(x2d, mix, w_up_b.reshape(D_MODEL, D_FF), w_down_b.reshape(D_FF, D_MODEL),
              g0, b0, row(ln1_g[0]), row(ln1_b[0]), row(ln2_g[0]), row(ln2_b[0]), tm=512, tf=1024)
    return h2.reshape(nb, seq, d)
```

```python
import functools
import math

import jax
import jax.numpy as jnp
import numpy as np
from jax import lax
from jax.experimental import pallas as pl
from jax.experimental.pallas import tpu as pltpu

F32 = jnp.float32
BF16 = jnp.bfloat16

D_MODEL = 2048
N_META = 16
CHUNK = 128
PAD = CHUNK - N_META
D_SSM = 2048
D_ATT = 2048
SSM_HEAD_DIM = 64
SSM_HEADS = 32
SSM_GROUPS = 8
SSM_HPG = 4
GROUP_W = SSM_HPG * SSM_HEAD_DIM
D_STATE = 128
CONV_K = 4
D_CONV = D_SSM + 2 * SSM_GROUPS * D_STATE
ATT_V_DIM = 128
ATT_HEADS = 16
ATT_QK_DIM = 64
D_FF = 4 * D_MODEL
DEPTH = 1
ALPHA = (2 * DEPTH) ** 0.25
LN_EPS = 1e-5
RMS_EPS = 1e-5
NEG_INF = -1e30
LOG2E = math.log2(math.e)
LAMBDA_INIT = 0.8 - 0.6 * math.exp(-0.3 * 0)

N_PROJ = D_SSM + D_CONV + 3 * D_ATT
COL_Z = 0
COL_X = D_SSM
COL_B = COL_X + D_SSM
COL_C = COL_B + SSM_GROUPS * D_STATE
COL_Q = COL_C + SSM_GROUPS * D_STATE
COL_K = COL_Q + D_ATT
COL_V = COL_K + D_ATT
W_IN_DT = D_SSM + D_CONV
DT_ROWS = 8

V7X_VMEM_BYTES = 64 * 1024 * 1024
VMEM_LIMIT = 56 * 1024 * 1024

ATT_TQ = 512
MLP_ACC_SLAB = 512


def _layer_norm_rows(x, g, b):
    mu = jnp.mean(x, axis=-1, keepdims=True)
    xc = x - mu
    var = jnp.mean(xc * xc, axis=-1, keepdims=True)
    return xc * lax.rsqrt(var + LN_EPS) * g + b


def _silu(x):
    hx = 0.5 * x
    return hx + hx * jnp.tanh(hx)


def _softplus(x):
    return jnp.maximum(x, 0.0) + jnp.log1p(jnp.exp(-jnp.abs(x)))


CONTRACT_LAST = (((1,), (1,)), ((), ()))


def _ln0_kernel(x_ref, xm_ref, g_ref, b_ref, wdt_ref, hn_ref, dt_ref, hm_ref, dtm_ref, *, ln_rows):
    def body(r, carry):
        r0 = pl.multiple_of(r * ln_rows, ln_rows)
        hn = _layer_norm_rows(x_ref[pl.ds(r0, ln_rows), :], g_ref[...], b_ref[...])
        hn_ref[pl.ds(r0, ln_rows), :] = hn.astype(BF16)
        return carry

    lax.fori_loop(0, x_ref.shape[0] // ln_rows, body, 0)
    dt_ref[...] = lax.dot_general(wdt_ref[...], hn_ref[...], CONTRACT_LAST, preferred_element_type=F32)

    @pl.when(pl.program_id(0) == 0)
    def _():
        hm_ref[...] = _layer_norm_rows(xm_ref[...], g_ref[...], b_ref[...]).astype(BF16)
        dtm_ref[...] = lax.dot_general(wdt_ref[...], hm_ref[...], CONTRACT_LAST, preferred_element_type=F32)


def _ln0(x2d, meta_pad, g, b, wdt, *, tm):
    m = x2d.shape[0]
    ndt = wdt.shape[0]
    const = lambda i: (0, 0)
    return pl.pallas_call(
        functools.partial(_ln0_kernel, ln_rows=128),
        grid=(m // tm,),
        in_specs=[
            pl.BlockSpec((tm, D_MODEL), lambda i: (i, 0)),
            pl.BlockSpec((CHUNK, D_MODEL), const),
            pl.BlockSpec((1, D_MODEL), const),
            pl.BlockSpec((1, D_MODEL), const),
            pl.BlockSpec((ndt, D_MODEL), const),
        ],
        out_specs=[
            pl.BlockSpec((tm, D_MODEL), lambda i: (i, 0)),
            pl.BlockSpec((ndt, tm), lambda i: (0, i)),
            pl.BlockSpec((CHUNK, D_MODEL), const),
            pl.BlockSpec((ndt, CHUNK), const),
        ],
        out_shape=[
            jax.ShapeDtypeStruct((m, D_MODEL), BF16),
            jax.ShapeDtypeStruct((ndt, m), F32),
            jax.ShapeDtypeStruct((CHUNK, D_MODEL), BF16),
            jax.ShapeDtypeStruct((ndt, CHUNK), F32),
        ],
        compiler_params=pltpu.CompilerParams(dimension_semantics=("arbitrary",), vmem_limit_bytes=VMEM_LIMIT),
        name="ln0",
    )(x2d, meta_pad, g, b, wdt)


def _inproj_kernel(hn_ref, hm_ref, wt_ref, o_ref, om_ref, wb_ref):
    @pl.when(pl.program_id(1) == 0)
    def _():
        wb_ref[...] = wt_ref[...].reshape(wb_ref.shape).astype(BF16)
        om_ref[...] = lax.dot_general(hm_ref[...], wb_ref[...], CONTRACT_LAST,
                                      preferred_element_type=F32).astype(om_ref.dtype)

    o_ref[...] = lax.dot_general(hn_ref[...], wb_ref[...], CONTRACT_LAST,
                                 preferred_element_type=F32).astype(o_ref.dtype)


def _inproj(hn, hm, wt, *, tm, tn):
    m = hn.shape[0]
    n = N_PROJ
    assert W_IN_DT % tn == 0 and n % tn == 0 and tn % SSM_HEADS == 0 and wt.shape[1] == SSM_HEADS
    groups = tn // SSM_HEADS
    group_start = lambda j: j * groups + (j >= W_IN_DT // tn).astype(jnp.int32)
    return pl.pallas_call(
        _inproj_kernel,
        grid=(n // tn, m // tm),
        in_specs=[
            pl.BlockSpec((tm, D_MODEL), lambda j, i: (i, 0)),
            pl.BlockSpec((CHUNK, D_MODEL), lambda j, i: (0, 0)),
            pl.BlockSpec((pl.Element(groups), pl.Element(SSM_HEADS), pl.Element(D_MODEL)),
                         lambda j, i: (group_start(j), 0, 0)),
        ],
        out_specs=[
            pl.BlockSpec((tm, tn), lambda j, i: (i, j)),
            pl.BlockSpec((CHUNK, tn), lambda j, i: (0, j)),
        ],
        out_shape=[
            jax.ShapeDtypeStruct((m, n), BF16),
            jax.ShapeDtypeStruct((CHUNK, n), BF16),
        ],
        scratch_shapes=[pltpu.VMEM((tn, D_MODEL), BF16)],
        compiler_params=pltpu.CompilerParams(
            dimension_semantics=("arbitrary", "arbitrary"), vmem_limit_bytes=VMEM_LIMIT),
        name="inproj",
    )(hn, hm, wt)


U_LEAD = 16
U_SEQ0 = U_LEAD + CHUNK


def _conv_silu(u_ref, r0, w_ref, b_ref):
    win = u_ref[pl.ds(r0 - 8, CHUNK + 8), :]
    w = w_ref[...]
    acc = b_ref[...] + w[3:4, :] * win[8:CHUNK + 8, :]
    acc = acc + w[2:3, :] * win[7:CHUNK + 7, :]
    acc = acc + w[1:2, :] * win[6:CHUNK + 6, :]
    acc = acc + w[0:1, :] * win[5:CHUNK + 5, :]
    return _silu(acc)


def _ssd_kernel(xs_ref, b_ref, c_ref, z_ref, xm_ref, bm_ref, cm_ref, dt_ref, dtm_ref,
                cwx_ref, cwb_ref, cwc_ref, cbx_ref, cbb_ref, cbc_ref,
                dtb_ref, alog_ref, dskip_ref, nw_ref,
                y_ref, ux_ref, ub_ref, uc_ref, st_ref, m1_ref, tril_ref, m2_ref, hm_ref):
    seq = xs_ref.shape[1]
    n_chunks = seq // CHUNK

    li = lax.broadcasted_iota(jnp.int32, (CHUNK, CHUNK), 0)
    ki = lax.broadcasted_iota(jnp.int32, (CHUNK, CHUNK), 1)
    lower = jnp.where(ki <= li, 1.0, 0.0)
    tril_ref[...] = lower
    m1_ref[...] = lower.astype(BF16)
    k2 = lax.broadcasted_iota(jnp.int32, (2 * CHUNK, 2 * CHUNK), 0) & (CHUNK - 1)
    s2 = lax.broadcasted_iota(jnp.int32, (2 * CHUNK, 2 * CHUNK), 1)
    m2_ref[...] = jnp.where((s2 >= CHUNK) | (k2 > s2), 1.0, 0.0).astype(BF16)
    head_of = lax.broadcasted_iota(jnp.int32, (CHUNK, GROUP_W), 1) >> 6
    for r in range(SSM_HPG):
        hm_ref[r] = jnp.where(head_of == r, 1.0, 0.0).astype(BF16)

    row = lax.broadcasted_iota(jnp.int32, (CHUNK, 1), 0)
    meta_valid = row >= PAD
    for u_ref, m_ref, s_ref in ((ux_ref, xm_ref, xs_ref), (ub_ref, bm_ref, b_ref), (uc_ref, cm_ref, c_ref)):
        ncol = u_ref.shape[1]
        u_ref[0:U_LEAD, :] = jnp.zeros((U_LEAD, ncol), F32)
        u_ref[U_LEAD:U_SEQ0, :] = jnp.where(meta_valid, m_ref[...].astype(F32), 0.0)
        u_ref[U_SEQ0:U_SEQ0 + seq, :] = s_ref[0].astype(F32)
    st_ref[...] = jnp.zeros(st_ref.shape, F32)

    dt_bias = dtb_ref[0]
    a_neg = -jnp.exp(alog_ref[0])

    def chunk_step(r0, dt8, z_rows, out_row0):
        xc = _conv_silu(ux_ref, r0, cwx_ref, cbx_ref)
        bc = _conv_silu(ub_ref, r0, cwb_ref, cbb_ref)
        a8 = dt8 * a_neg
        xs_bf = xc.astype(BF16)
        bt = bc.T
        st_old = st_ref[...]
        want_y = z_rows is not None
        if want_y:
            cc = _conv_silu(uc_ref, r0, cwc_ref, cbc_ref)
            cb = lax.dot_general(cc.astype(BF16), bc.astype(BF16), CONTRACT_LAST,
                                 preferred_element_type=F32)
            cb_low = cb * tril_ref[...]
            st_bf = st_old.astype(BF16)
        lhs_y, rhs_y, lhs_s, rhs_s, tot = [], [], [], [], []
        for r in range(SSM_HPG):
            a_r = a8[r:r + 1, :]
            dt_r = dt8[r:r + 1, :]
            a_hi = a_r.astype(BF16)
            a_lo = (a_r - a_hi.astype(F32)).astype(BF16)
            m1 = m1_ref[...]
            dd = jnp.dot(jnp.concatenate([m1 * a_hi, m1 * a_lo], axis=1), m2_ref[...],
                         preferred_element_type=F32)
            dseg = dd[:, :CHUNK]
            acsb = dd[:, CHUNK:]
            lmat = jnp.exp(dseg)
            xs_r = xs_bf * hm_ref[r]
            if want_y:
                lhs_y.append((cb_low * lmat * dt_r).astype(BF16))
                lhs_y.append((cc * jnp.exp(acsb)).astype(BF16))
                rhs_y.append(xs_r)
                rhs_y.append(st_bf * hm_ref[r])
            f1 = lmat[CHUNK - 1:CHUNK, :] * dt_r
            lhs_s.append((bt * f1).astype(BF16))
            rhs_s.append(xs_r)
            tot.append(acsb[CHUNK - 1:CHUNK, :])
        s_new = jnp.dot(jnp.concatenate(lhs_s, axis=1), jnp.concatenate(rhs_s, axis=0),
                        preferred_element_type=F32)
        half = lax.broadcasted_iota(jnp.int32, (1, CHUNK), 1) < SSM_HEAD_DIM
        decay = jnp.exp(jnp.concatenate([jnp.where(half, tot[0], tot[1]),
                                         jnp.where(half, tot[2], tot[3])], axis=1))
        st_ref[...] = st_old * decay + s_new
        if want_y:
            y = jnp.dot(jnp.concatenate(lhs_y, axis=1), jnp.concatenate(rhs_y, axis=0),
                        preferred_element_type=F32)
            y = y + xc * dskip_ref[...]
            gated = y * _silu(z_rows.astype(F32))
            ms = jnp.mean(gated * gated, axis=-1, keepdims=True)
            y_ref[0, pl.ds(out_row0, CHUNK), :] = (gated * lax.rsqrt(ms + RMS_EPS) * nw_ref[...]).astype(y_ref.dtype)

    lane_pos = lax.broadcasted_iota(jnp.int32, (1, CHUNK), 1)
    dt_meta = jnp.where(lane_pos >= PAD, _softplus(dtm_ref[0] + dt_bias), 0.0)
    chunk_step(U_LEAD, dt_meta, None, None)

    def body(s, carry):
        row0 = pl.multiple_of(s * CHUNK, CHUNK)
        dt8 = _softplus(dt_ref[0, :, pl.ds(row0, CHUNK)] + dt_bias)
        chunk_step(U_SEQ0 + row0, dt8, z_ref[0, pl.ds(row0, CHUNK), :], row0)
        return carry

    lax.fori_loop(0, n_chunks, body, 0, unroll=8)


def _ssd(proj3, proj_meta, dt3, dt3_meta, conv_w, conv_b2, dtb3, alog3, dskip_row, nw_row):
    nb, seq, _ = proj3.shape
    gx, gb, gc = COL_X // GROUP_W, COL_B // D_STATE, COL_C // D_STATE
    u_rows = U_SEQ0 + seq
    return pl.pallas_call(
        _ssd_kernel,
        grid=(nb, SSM_GROUPS),
        in_specs=[
            pl.BlockSpec((1, seq, GROUP_W), lambda b, g: (b, 0, gx + g)),
            pl.BlockSpec((1, seq, D_STATE), lambda b, g: (b, 0, gb + g)),
            pl.BlockSpec((1, seq, D_STATE), lambda b, g: (b, 0, gc + g)),
            pl.BlockSpec((1, seq, GROUP_W), lambda b, g: (b, 0, g)),
            pl.BlockSpec((CHUNK, GROUP_W), lambda b, g: (0, gx + g)),
            pl.BlockSpec((CHUNK, D_STATE), lambda b, g: (0, gb + g)),
            pl.BlockSpec((CHUNK, D_STATE), lambda b, g: (0, gc + g)),
            pl.BlockSpec((1, DT_ROWS, seq), lambda b, g: (g, 0, b)),
            pl.BlockSpec((1, DT_ROWS, CHUNK), lambda b, g: (g, 0, 0)),
            pl.BlockSpec((CONV_K, GROUP_W), lambda b, g: (0, g)),
            pl.BlockSpec((CONV_K, D_STATE), lambda b, g: (0, D_SSM // D_STATE + g)),
            pl.BlockSpec((CONV_K, D_STATE), lambda b, g: (0, D_SSM // D_STATE + SSM_GROUPS + g)),
            pl.BlockSpec((1, GROUP_W), lambda b, g: (0, g)),
            pl.BlockSpec((1, D_STATE), lambda b, g: (0, D_SSM // D_STATE + g)),
            pl.BlockSpec((1, D_STATE), lambda b, g: (0, D_SSM // D_STATE + SSM_GROUPS + g)),
            pl.BlockSpec((1, DT_ROWS, 1), lambda b, g: (g, 0, 0)),
            pl.BlockSpec((1, DT_ROWS, 1), lambda b, g: (g, 0, 0)),
            pl.BlockSpec((1, GROUP_W), lambda b, g: (0, g)),
            pl.BlockSpec((1, GROUP_W), lambda b, g: (0, g)),
        ],
        out_specs=pl.BlockSpec((1, seq, GROUP_W), lambda b, g: (b, 0, g)),
        out_shape=jax.ShapeDtypeStruct((nb, seq, D_SSM), BF16),
        scratch_shapes=[
            pltpu.VMEM((u_rows, GROUP_W), F32),
            pltpu.VMEM((u_rows, D_STATE), F32),
            pltpu.VMEM((u_rows, D_STATE), F32),
            pltpu.VMEM((D_STATE, GROUP_W), F32),
            pltpu.VMEM((CHUNK, CHUNK), BF16),
            pltpu.VMEM((CHUNK, CHUNK), F32),
            pltpu.VMEM((2 * CHUNK, 2 * CHUNK), BF16),
            pltpu.VMEM((SSM_HPG, CHUNK, GROUP_W), BF16),
        ],
        compiler_params=pltpu.CompilerParams(
            dimension_semantics=("arbitrary", "arbitrary"), vmem_limit_bytes=VMEM_LIMIT),
        name="ssd",
    )(proj3, proj3, proj3, proj3, proj_meta, proj_meta, proj_meta, dt3, dt3_meta,
      conv_w, conv_w, conv_w, conv_b2, conv_b2, conv_b2, dtb3, alog3, dskip_row, nw_row)


def _attn_kernel(slope_ref, q_ref, k_ref, v_ref, km_ref, vm_ref, lam_ref, nw_ref, wu_ref, wd_ref,
                 o_ref, wub_ref, wdb_ref, kaug_ref, vt_ref, s_ref, p_ref):
    wub_ref[...] = wu_ref[...].astype(BF16)
    wdb_ref[...] = wd_ref[...].astype(BF16)

    h = pl.program_id(1)
    seq = q_ref.shape[1]
    tq = ATT_TQ
    slope = slope_ref[h]

    sub8 = lax.broadcasted_iota(jnp.int32, (8, CHUNK), 0)
    lane8 = lax.broadcasted_iota(jnp.int32, (8, CHUNK), 1)

    def stage(r0, k, v, valid_from):
        pos = lane8 + r0
        bias = (slope * LOG2E) * pos.astype(F32)
        if valid_from:
            bias = jnp.where(pos >= valid_from, bias, NEG_INF)
        b1 = bias.astype(BF16).astype(F32)
        b2 = (bias - b1).astype(BF16).astype(F32)
        b3 = bias - b1 - b2
        rows8 = jnp.where(sub8 == 0, b1, jnp.where(sub8 == 1, b2, jnp.where(sub8 == 2, b3, 0.0)))
        cols = jnp.concatenate([rows8, jnp.zeros((CHUNK - 8, CHUNK), F32)], axis=0).T
        kaug_ref[r0:r0 + CHUNK, 0:128] = (k.astype(F32) * LOG2E).astype(BF16)
        kaug_ref[r0:r0 + CHUNK, 128:256] = cols.astype(BF16)
        vt_ref[0:ATT_V_DIM, r0:r0 + CHUNK] = v.T

    vt_ref[ATT_V_DIM:, :] = jnp.ones((vt_ref.shape[0] - ATT_V_DIM, vt_ref.shape[1]), BF16)
    stage(0, km_ref[...], vm_ref[...], PAD)
    for j in range(seq // CHUNK):
        rows = slice(CHUNK * j, CHUNK * (j + 1))
        stage(CHUNK * (j + 1), k_ref[0, rows, :], v_ref[0, rows, :], 0)

    lam4 = lam_ref[...]
    lam = (jnp.exp(jnp.sum(lam4[0:1] * lam4[1:2], axis=-1, keepdims=True))
           - jnp.exp(jnp.sum(lam4[2:3] * lam4[3:4], axis=-1, keepdims=True)) + LAMBDA_INIT)

    lane_q = lax.broadcasted_iota(jnp.int32, (tq, 128), 1)
    ones_blk = jnp.where(lane_q < 3, 1.0, 0.0).astype(BF16)
    map_lanes = (lane_q < ATT_QK_DIM, lane_q >= ATT_QK_DIM)
    hq = tq // 2
    tri = (lax.broadcasted_iota(jnp.int32, (hq, hq), 0)
           <= lax.broadcasted_iota(jnp.int32, (hq, hq), 1))

    def scores(rows, qa):
        return lax.dot_general(kaug_ref[rows, :], qa, CONTRACT_LAST, preferred_element_type=F32)

    class Item:
        def __init__(self, t, c):
            self.t, self.c = t, c
            d0 = CHUNK + tq * t
            self.kend = d0 + tq
            self.rows_a, self.rows_b = slice(d0, d0 + hq), slice(d0 + hq, self.kend)
            self.blocks = [slice(0, CHUNK)] + [slice(CHUNK + tq * j, CHUNK + tq * (j + 1)) for j in range(t)]
            self.m = None
            self.qaug = None

        def score_steps(self):
            c = self.c

            def first():
                q = q_ref[0, tq * self.t:tq * (self.t + 1), :] * jnp.asarray(ATT_QK_DIM ** -0.5, BF16)
                self.qaug = jnp.concatenate([jnp.where(map_lanes[c], q, jnp.zeros_like(q)), ones_blk], axis=1)

            def full(rows):
                def step():
                    sv = scores(rows, self.qaug)
                    s_ref[c, rows, :] = sv
                    bm = jnp.max(sv, axis=0, keepdims=True)
                    self.m = bm if self.m is None else jnp.maximum(self.m, bm)
                return step

            def diag_a():
                sv = scores(self.rows_a, self.qaug)
                sv = jnp.concatenate([jnp.where(tri, sv[:, :hq], NEG_INF), sv[:, hq:]], axis=1)
                s_ref[c, self.rows_a, :] = sv
                self.m = jnp.maximum(self.m, jnp.max(sv, axis=0, keepdims=True))

            def diag_b():
                sv = jnp.where(tri, scores(self.rows_b, self.qaug[hq:, :]), NEG_INF)
                s_ref[c, self.rows_b, hq:] = sv
                m = self.m
                self.m = jnp.concatenate(
                    [m[:, :hq], jnp.maximum(m[:, hq:], jnp.max(sv, axis=0, keepdims=True))], axis=1)

            return [first] + [full(r) for r in self.blocks] + [diag_a, diag_b]

        def prob_steps(self):
            c = self.c

            def full(rows):
                def step():
                    p_ref[c, rows, :] = jnp.exp2(s_ref[c, rows, :] - self.m).astype(BF16)
                return step

            def diag_b():
                p_ref[c, self.rows_b, :hq] = jnp.zeros((hq, hq), BF16)
                p_ref[c, self.rows_b, hq:] = jnp.exp2(s_ref[c, self.rows_b, hq:] - self.m[:, hq:]).astype(BF16)

            return [full(r) for r in self.blocks + [self.rows_a]] + [diag_b]

        def value_step(self):
            acc = jnp.dot(vt_ref[:, 0:self.kend], p_ref[self.c, 0:self.kend, :],
                          preferred_element_type=F32)
            return acc[:ATT_V_DIM] / acc[ATT_V_DIM:ATT_V_DIM + 1]

    items = [Item(t, c) for t in range(seq // tq) for c in range(2)]
    n_items = len(items)
    outs = {}

    def finish(i):
        item = items[i]
        outs[i] = item.value_step()
        if item.c == 1:
            o = outs.pop(i - 1) - lam * outs.pop(i)
            ms = jnp.mean(o * o, axis=0, keepdims=True)
            o = o * lax.rsqrt(ms + RMS_EPS) * nw_ref[...] * (1.0 - LAMBDA_INIT)
            o_ref[0, tq * item.t:tq * (item.t + 1), :] = o.T.astype(o_ref.dtype)

    for stage in range(-2, n_items):
        lanes = []
        if 0 <= stage + 2 < n_items:
            lanes.append(items[stage + 2].score_steps())
        if 0 <= stage + 1 < n_items:
            lanes.append(items[stage + 1].prob_steps())
        if 0 <= stage:
            lanes.append([functools.partial(finish, stage)])
        for k in range(max(len(steps) for steps in lanes)):
            for steps in lanes:
                if k < len(steps):
                    steps[k]()


def _attention(slopes, proj3, proj_meta, lam4, nw_col, w_up, w_down):
    nb, seq, _ = proj3.shape
    steps = nb * ATT_HEADS
    wu3 = w_up.reshape(steps, D_MODEL // steps, D_FF)
    wd3 = w_down.reshape(steps, D_FF // steps, D_MODEL)
    w_blk = lambda a: pl.BlockSpec((1,) + a.shape[1:], lambda b, h: (b * ATT_HEADS + h, 0, 0))
    cq, ck, cv = COL_Q // ATT_V_DIM, COL_K // ATT_V_DIM, COL_V // ATT_V_DIM
    kv_rows = CHUNK + seq
    seq_blk = lambda c0: pl.BlockSpec((1, seq, ATT_V_DIM), lambda b, h: (b, 0, c0 + h))
    meta_blk = lambda c0: pl.BlockSpec((CHUNK, ATT_V_DIM), lambda b, h: (0, c0 + h))
    return pl.pallas_call(
        _attn_kernel,
        grid=(nb, ATT_HEADS),
        in_specs=[
            pl.BlockSpec(memory_space=pltpu.SMEM),
            seq_blk(cq), seq_blk(ck), seq_blk(cv), meta_blk(ck), meta_blk(cv),
            pl.BlockSpec((4, ATT_QK_DIM), lambda b, h: (0, 0)),
            pl.BlockSpec((ATT_V_DIM, 1), lambda b, h: (0, 0)),
            w_blk(wu3), w_blk(wd3),
        ],
        out_specs=[pl.BlockSpec((1, seq, ATT_V_DIM), lambda b, h: (b, 0, h)), w_blk(wu3), w_blk(wd3)],
        out_shape=[jax.ShapeDtypeStruct((nb, seq, D_ATT), BF16),
                   jax.ShapeDtypeStruct(wu3.shape, BF16), jax.ShapeDtypeStruct(wd3.shape, BF16)],
        scratch_shapes=[
            pltpu.VMEM((kv_rows, 2 * ATT_V_DIM), BF16),
            pltpu.VMEM((ATT_V_DIM + 16, kv_rows), BF16),
            pltpu.VMEM((2, kv_rows, ATT_TQ), F32),
            pltpu.VMEM((2, kv_rows, ATT_TQ), BF16),
        ],
        compiler_params=pltpu.CompilerParams(
            dimension_semantics=("arbitrary", "arbitrary"), vmem_limit_bytes=VMEM_LIMIT),
        name="diffattn",
    )(slopes, proj3, proj3, proj3, proj_meta, proj_meta, lam4, nw_col, wu3, wd3)


def _outproj_kernel(y_ref, o_ref, wy_ref, wo_ref, mix_ref, wyb_ref, wob_ref):
    @pl.when(pl.program_id(1) == 0)
    def _():
        wyb_ref[...] = wy_ref[...].astype(BF16)
        wob_ref[...] = wo_ref[...].astype(BF16)

    mix_ref[...] = (jnp.dot(y_ref[...], wyb_ref[...], preferred_element_type=F32)
                    + jnp.dot(o_ref[...], wob_ref[...], preferred_element_type=F32))


def _outproj(y2d, o2d, w_out, *, tm, tn):
    m = y2d.shape[0]
    return pl.pallas_call(
        _outproj_kernel,
        grid=(D_MODEL // tn, m // tm),
        in_specs=[
            pl.BlockSpec((tm, D_SSM), lambda j, i: (i, 0)),
            pl.BlockSpec((tm, D_ATT), lambda j, i: (i, 0)),
            pl.BlockSpec((D_SSM, tn), lambda j, i: (0, j)),
            pl.BlockSpec((D_ATT, tn), lambda j, i: (1, j)),
        ],
        out_specs=pl.BlockSpec((tm, tn), lambda j, i: (i, j)),
        out_shape=jax.ShapeDtypeStruct((m, D_MODEL), F32),
        scratch_shapes=[pltpu.VMEM((D_SSM, tn), BF16), pltpu.VMEM((D_ATT, tn), BF16)],
        compiler_params=pltpu.CompilerParams(
            dimension_semantics=("arbitrary", "arbitrary"), vmem_limit_bytes=VMEM_LIMIT),
        name="outproj",
    )(y2d, o2d, w_out, w_out)


def _mlp_kernel(x_ref, mix_ref, wu_ref, wd_ref, g0_ref, b0_ref, g1_ref, b1_ref, g2_ref, b2_ref,
                o_ref, h1_ref, hb_ref, *, ln_rows):
    f = pl.program_id(1)
    n_ln = x_ref.shape[0] // ln_rows

    @pl.when(f == 0)
    def _():
        def body(r, carry):
            rows = pl.ds(pl.multiple_of(r * ln_rows, ln_rows), ln_rows)
            h0 = _layer_norm_rows(x_ref[rows, :], g0_ref[...], b0_ref[...])
            h1 = _layer_norm_rows(ALPHA * h0 + mix_ref[rows, :], g1_ref[...], b1_ref[...])
            h1_ref[rows, :] = h1
            hb_ref[rows, :] = h1.astype(BF16)
            return carry

        lax.fori_loop(0, n_ln, body, 0)
        o_ref[...] = jnp.zeros(o_ref.shape, F32)

    u = jnp.dot(hb_ref[...], wu_ref[...], preferred_element_type=F32)
    u = jnp.square(jnp.maximum(u, 0.0)).astype(BF16)
    for c0 in range(0, D_MODEL, MLP_ACC_SLAB):
        cols = slice(c0, c0 + MLP_ACC_SLAB)
        o_ref[:, cols] += jnp.dot(u, wd_ref[:, cols], preferred_element_type=F32)

    @pl.when(f == pl.num_programs(1) - 1)
    def _():
        def body(r, carry):
            rows = pl.ds(pl.multiple_of(r * ln_rows, ln_rows), ln_rows)
            o_ref[rows, :] = _layer_norm_rows(ALPHA * h1_ref[rows, :] + o_ref[rows, :], g2_ref[...], b2_ref[...])
            return carry

        lax.fori_loop(0, n_ln, body, 0)


def _mlp(x2d, mix, w_up, w_down, g0, b0, g1, b1, g2, b2, *, tm, tf):
    m = x2d.shape[0]
    row = pl.BlockSpec((1, D_MODEL), lambda i, f: (0, 0))
    return pl.pallas_call(
        functools.partial(_mlp_kernel, ln_rows=128),
        grid=(m // tm, D_FF // tf),
        in_specs=[
            pl.BlockSpec((tm, D_MODEL), lambda i, f: (i, 0)),
            pl.BlockSpec((tm, D_MODEL), lambda i, f: (i, 0)),
            pl.BlockSpec((D_MODEL, tf), lambda i, f: (0, f)),
            pl.BlockSpec((tf, D_MODEL), lambda i, f: (f, 0)),
            row, row, row, row, row, row,
        ],
        out_specs=pl.BlockSpec((tm, D_MODEL), lambda i, f: (i, 0)),
        out_shape=jax.ShapeDtypeStruct((m, D_MODEL), F32),
        scratch_shapes=[pltpu.VMEM((tm, D_MODEL), F32), pltpu.VMEM((tm, D_MODEL), BF16)],
        compiler_params=pltpu.CompilerParams(
            dimension_semantics=("arbitrary", "arbitrary"), vmem_limit_bytes=VMEM_LIMIT),
        name="mlp",
    )(x2d, mix, w_up, w_down, g0, b0, g1, b1, g2, b2)


def kernel(x, meta_tokens, ln0_g, ln0_b, w_in, conv_w, conv_b, dt_bias, a_log, d_skip, ssd_norm_w,
           lambda_q1, lambda_k1, lambda_q2, lambda_k2, attn_norm_w, w_out, ln1_g, ln1_b, w_up, w_down,
           ln2_g, ln2_b):
    nb, seq, d = x.shape
    assert (d, w_in.shape[0]) == (D_MODEL, DEPTH) and seq % ATT_TQ == 0
    m = nb * seq
    x2d = x.reshape(m, d)
    row = lambda v: v.reshape(1, -1).astype(F32)

    w_in_t = w_in[0].astype(F32).T
    wdt = w_in_t[W_IN_DT:W_IN_DT + SSM_HEADS].reshape(SSM_GROUPS, SSM_HPG, d)
    wdt = jnp.pad(wdt, ((0, 0), (0, DT_ROWS - SSM_HPG), (0, 0))).reshape(SSM_GROUPS * DT_ROWS, d).astype(BF16)
    pad_heads = lambda v: jnp.pad(v.reshape(SSM_GROUPS, SSM_HPG).astype(F32),
                                  ((0, 0), (0, DT_ROWS - SSM_HPG)))[..., None]
    dtb3, alog3 = pad_heads(dt_bias[0]), pad_heads(a_log[0])
    dskip_row = row(jnp.repeat(d_skip[0], SSM_HEAD_DIM))
    lam4 = jnp.stack([lambda_q1[0], lambda_k1[0], lambda_q2[0], lambda_k2[0]]).astype(F32)
    slopes = jnp.asarray(2.0 ** (-8.0 * np.arange(1, ATT_HEADS + 1) / ATT_HEADS), dtype=F32)
    g0, b0 = row(ln0_g), row(ln0_b)
    meta_pad = jnp.pad(meta_tokens.astype(F32), ((PAD, 0), (0, 0)))

    hn, dt_t, hm, dt_t_meta = _ln0(x2d, meta_pad, g0, b0, wdt, tm=1024)
    proj, proj_meta = _inproj(hn, hm, w_in_t.reshape(-1, SSM_HEADS, d), tm=2048, tn=768)
    proj3 = proj.reshape(nb, seq, N_PROJ)
    dt3 = dt_t.reshape(SSM_GROUPS, DT_ROWS, m)
    dt3_meta = dt_t_meta.reshape(SSM_GROUPS, DT_ROWS, CHUNK)

    y = _ssd(proj3, proj_meta, dt3, dt3_meta, conv_w[0].astype(F32), row(conv_b[0]),
             dtb3, alog3, dskip_row, row(ssd_norm_w[0]))
    o, w_up_b, w_down_b = _attention(slopes, proj3, proj_meta, lam4,
                                     attn_norm_w[0].reshape(ATT_V_DIM, 1).astype(F32),
                                     w_up[0].astype(F32), w_down[0].astype(F32))

    mix = _outproj(y.reshape(m, D_SSM), o.reshape(m, D_ATT), w_out[0].astype(F32), tm=1024, tn=512)
    h2 = _mlp(x2d, mix, w_up_b.reshape(D_MODEL, D_FF), w_down_b.reshape(D_FF, D_MODEL),
              g0, b0, row(ln1_g[0]), row(ln1_b[0]), row(ln2_g[0]), row(ln2_b[0]), tm=512, tf=1024)
    return h2.reshape(nb, seq, d)
```

```python
import functools
import math

import jax
import jax.numpy as jnp
import numpy as np
from jax import lax
from jax.experimental import pallas as pl
from jax.experimental.pallas import tpu as pltpu

F32 = jnp.float32
BF16 = jnp.bfloat16

D_MODEL = 2048
N_META = 16
CHUNK = 128
PAD = CHUNK - N_META
D_SSM = 2048
D_ATT = 2048
SSM_HEAD_DIM = 64
SSM_HEADS = 32
SSM_GROUPS = 8
SSM_HPG = 4
GROUP_W = SSM_HPG * SSM_HEAD_DIM
D_STATE = 128
CONV_K = 4
D_CONV = D_SSM + 2 * SSM_GROUPS * D_STATE
ATT_V_DIM = 128
ATT_HEADS = 16
ATT_QK_DIM = 64
D_FF = 4 * D_MODEL
DEPTH = 1
ALPHA = (2 * DEPTH) ** 0.25
LN_EPS = 1e-5
RMS_EPS = 1e-5
NEG_INF = -1e30
LOG2E = math.log2(math.e)
LAMBDA_INIT = 0.8 - 0.6 * math.exp(-0.3 * 0)

N_PROJ = D_SSM + D_CONV + 3 * D_ATT
COL_Z = 0
COL_X = D_SSM
COL_B = COL_X + D_SSM
COL_C = COL_B + SSM_GROUPS * D_STATE
COL_Q = COL_C + SSM_GROUPS * D_STATE
COL_K = COL_Q + D_ATT
COL_V = COL_K + D_ATT
W_IN_DT = D_SSM + D_CONV
DT_ROWS = 8

V7X_VMEM_BYTES = 64 * 1024 * 1024
VMEM_LIMIT = 56 * 1024 * 1024

ATT_TQ = 512
MLP_ACC_SLAB = 512


def _layer_norm_rows(x, g, b):
    mu = jnp.mean(x, axis=-1, keepdims=True)
    xc = x - mu
    var = jnp.mean(xc * xc, axis=-1, keepdims=True)
    return xc * lax.rsqrt(var + LN_EPS) * g + b


def _silu(x):
    hx = 0.5 * x
    return hx + hx * jnp.tanh(hx)


def _softplus(x):
    return jnp.maximum(x, 0.0) + jnp.log1p(jnp.exp(-jnp.abs(x)))


CONTRACT_LAST = (((1,), (1,)), ((), ()))


def _ln0_kernel(x_ref, xm_ref, g_ref, b_ref, wdt_ref, hn_ref, dt_ref, hm_ref, dtm_ref, *, ln_rows):
    def body(r, carry):
        r0 = pl.multiple_of(r * ln_rows, ln_rows)
        hn = _layer_norm_rows(x_ref[pl.ds(r0, ln_rows), :], g_ref[...], b_ref[...])
        hn_ref[pl.ds(r0, ln_rows), :] = hn.astype(BF16)
        return carry

    lax.fori_loop(0, x_ref.shape[0] // ln_rows, body, 0)
    dt_ref[...] = lax.dot_general(wdt_ref[...], hn_ref[...], CONTRACT_LAST, preferred_element_type=F32)

    @pl.when(pl.program_id(0) == 0)
    def _():
        hm_ref[...] = _layer_norm_rows(xm_ref[...], g_ref[...], b_ref[...]).astype(BF16)
        dtm_ref[...] = lax.dot_general(wdt_ref[...], hm_ref[...], CONTRACT_LAST, preferred_element_type=F32)


def _ln0(x2d, meta_pad, g, b, wdt, *, tm):
    m = x2d.shape[0]
    ndt = wdt.shape[0]
    const = lambda i: (0, 0)
    return pl.pallas_call(
        functools.partial(_ln0_kernel, ln_rows=128),
        grid=(m // tm,),
        in_specs=[
            pl.BlockSpec((tm, D_MODEL), lambda i: (i, 0)),
            pl.BlockSpec((CHUNK, D_MODEL), const),
            pl.BlockSpec((1, D_MODEL), const),
            pl.BlockSpec((1, D_MODEL), const),
            pl.BlockSpec((ndt, D_MODEL), const),
        ],
        out_specs=[
            pl.BlockSpec((tm, D_MODEL), lambda i: (i, 0)),
            pl.BlockSpec((ndt, tm), lambda i: (0, i)),
            pl.BlockSpec((CHUNK, D_MODEL), const),
            pl.BlockSpec((ndt, CHUNK), const),
        ],
        out_shape=[
            jax.ShapeDtypeStruct((m, D_MODEL), BF16),
            jax.ShapeDtypeStruct((ndt, m), F32),
            jax.ShapeDtypeStruct((CHUNK, D_MODEL), BF16),
            jax.ShapeDtypeStruct((ndt, CHUNK), F32),
        ],
        compiler_params=pltpu.CompilerParams(dimension_semantics=("arbitrary",), vmem_limit_bytes=VMEM_LIMIT),
        name="ln0",
    )(x2d, meta_pad, g, b, wdt)


def _inproj_kernel(hn_ref, hm_ref, wt_ref, o_ref, om_ref, wb_ref):
    j = pl.program_id(0)
    tn = wb_ref.shape[0]
    new_tile = pl.program_id(1) == 0
    unused = (j < COL_X // tn) | ((j >= pl.cdiv(COL_Q, tn)) & (j < COL_K // tn))

    @pl.when(new_tile)
    def _():
        wb_ref[...] = wt_ref[...].reshape(wb_ref.shape).astype(BF16)

    @pl.when(new_tile & jnp.logical_not(unused))
    def _():
        om_ref[...] = lax.dot_general(hm_ref[...], wb_ref[...], CONTRACT_LAST,
                                      preferred_element_type=F32).astype(om_ref.dtype)

    @pl.when(new_tile & unused)
    def _():
        om_ref[...] = jnp.zeros(om_ref.shape, om_ref.dtype)

    o_ref[...] = lax.dot_general(hn_ref[...], wb_ref[...], CONTRACT_LAST,
                                 preferred_element_type=F32).astype(o_ref.dtype)


def _inproj(hn, hm, wt, *, tm, tn):
    m = hn.shape[0]
    n = N_PROJ
    assert W_IN_DT % tn == 0 and n % tn == 0 and tn % SSM_HEADS == 0 and wt.shape[1] == SSM_HEADS
    groups = tn // SSM_HEADS
    group_start = lambda j: j * groups + (j >= W_IN_DT // tn).astype(jnp.int32)
    return pl.pallas_call(
        _inproj_kernel,
        grid=(n // tn, m // tm),
        in_specs=[
            pl.BlockSpec((tm, D_MODEL), lambda j, i: (i, 0)),
            pl.BlockSpec((CHUNK, D_MODEL), lambda j, i: (0, 0)),
            pl.BlockSpec((pl.Element(groups), pl.Element(SSM_HEADS), pl.Element(D_MODEL)),
                         lambda j, i: (group_start(j), 0, 0)),
        ],
        out_specs=[
            pl.BlockSpec((tm, tn), lambda j, i: (i, j)),
            pl.BlockSpec((CHUNK, tn), lambda j, i: (0, j)),
        ],
        out_shape=[
            jax.ShapeDtypeStruct((m, n), BF16),
            jax.ShapeDtypeStruct((CHUNK, n), BF16),
        ],
        scratch_shapes=[pltpu.VMEM((tn, D_MODEL), BF16)],
        compiler_params=pltpu.CompilerParams(
            dimension_semantics=("arbitrary", "arbitrary"), vmem_limit_bytes=VMEM_LIMIT),
        name="inproj",
    )(hn, hm, wt)


U_LEAD = 16
U_SEQ0 = U_LEAD + CHUNK


def _conv_silu(u_ref, r0, w_ref, b_ref):
    win = u_ref[pl.ds(r0 - 8, CHUNK + 8), :]
    w = w_ref[...]
    acc = b_ref[...] + w[3:4, :] * win[8:CHUNK + 8, :]
    acc = acc + w[2:3, :] * win[7:CHUNK + 7, :]
    acc = acc + w[1:2, :] * win[6:CHUNK + 6, :]
    acc = acc + w[0:1, :] * win[5:CHUNK + 5, :]
    return _silu(acc)


def _ssd_kernel(xs_ref, b_ref, c_ref, z_ref, xm_ref, bm_ref, cm_ref, dt_ref, dtm_ref,
                cwx_ref, cwb_ref, cwc_ref, cbx_ref, cbb_ref, cbc_ref,
                dtb_ref, alog_ref, dskip_ref, nw_ref,
                y_ref, ux_ref, ub_ref, uc_ref, st_ref, m1_ref, tril_ref, m2_ref, hm_ref):
    seq = xs_ref.shape[1]
    n_chunks = seq // CHUNK

    li = lax.broadcasted_iota(jnp.int32, (CHUNK, CHUNK), 0)
    ki = lax.broadcasted_iota(jnp.int32, (CHUNK, CHUNK), 1)
    lower = jnp.where(ki <= li, 1.0, 0.0)
    tril_ref[...] = lower
    m1_ref[...] = lower.astype(BF16)
    k2 = lax.broadcasted_iota(jnp.int32, (2 * CHUNK, 2 * CHUNK), 0) & (CHUNK - 1)
    s2 = lax.broadcasted_iota(jnp.int32, (2 * CHUNK, 2 * CHUNK), 1)
    m2_ref[...] = jnp.where((s2 >= CHUNK) | (k2 > s2), 1.0, 0.0).astype(BF16)
    head_of = lax.broadcasted_iota(jnp.int32, (CHUNK, GROUP_W), 1) >> 6
    for r in range(SSM_HPG):
        hm_ref[r] = jnp.where(head_of == r, 1.0, 0.0).astype(BF16)

    row = lax.broadcasted_iota(jnp.int32, (CHUNK, 1), 0)
    meta_valid = row >= PAD
    for u_ref, m_ref, s_ref in ((ux_ref, xm_ref, xs_ref), (ub_ref, bm_ref, b_ref), (uc_ref, cm_ref, c_ref)):
        ncol = u_ref.shape[1]
        u_ref[0:U_LEAD, :] = jnp.zeros((U_LEAD, ncol), F32)
        u_ref[U_LEAD:U_SEQ0, :] = jnp.where(meta_valid, m_ref[...].astype(F32), 0.0)
        u_ref[U_SEQ0:U_SEQ0 + seq, :] = s_ref[0].astype(F32)
    st_ref[...] = jnp.zeros(st_ref.shape, F32)

    dt_bias = dtb_ref[0]
    a_neg = -jnp.exp(alog_ref[0])

    def chunk_step(r0, dt8, z_rows, out_row0):
        xc = _conv_silu(ux_ref, r0, cwx_ref, cbx_ref)
        bc = _conv_silu(ub_ref, r0, cwb_ref, cbb_ref)
        a8 = dt8 * a_neg
        xs_bf = xc.astype(BF16)
        bt = bc.T
        st_old = st_ref[...]
        want_y = z_rows is not None
        if want_y:
            cc = _conv_silu(uc_ref, r0, cwc_ref, cbc_ref)
            cb = lax.dot_general(cc.astype(BF16), bc.astype(BF16), CONTRACT_LAST,
                                 preferred_element_type=F32)
            cb_low = cb * tril_ref[...]
            st_bf = st_old.astype(BF16)
        lhs_y, rhs_y, lhs_s, rhs_s, tot = [], [], [], [], []
        for r in range(SSM_HPG):
            a_r = a8[r:r + 1, :]
            dt_r = dt8[r:r + 1, :]
            a_hi = a_r.astype(BF16)
            a_lo = (a_r - a_hi.astype(F32)).astype(BF16)
            m1 = m1_ref[...]
            dd = jnp.dot(jnp.concatenate([m1 * a_hi, m1 * a_lo], axis=1), m2_ref[...],
                         preferred_element_type=F32)
            dseg = dd[:, :CHUNK]
            acsb = dd[:, CHUNK:]
            lmat = jnp.exp(dseg)
            xs_r = xs_bf * hm_ref[r]
            if want_y:
                lhs_y.append((cb_low * lmat * dt_r).astype(BF16))
                lhs_y.append((cc * jnp.exp(acsb)).astype(BF16))
                rhs_y.append(xs_r)
                rhs_y.append(st_bf * hm_ref[r])
            f1 = lmat[CHUNK - 1:CHUNK, :] * dt_r
            lhs_s.append((bt * f1).astype(BF16))
            rhs_s.append(xs_r)
            tot.append(acsb[CHUNK - 1:CHUNK, :])
        s_new = jnp.dot(jnp.concatenate(lhs_s, axis=1), jnp.concatenate(rhs_s, axis=0),
                        preferred_element_type=F32)
        half = lax.broadcasted_iota(jnp.int32, (1, CHUNK), 1) < SSM_HEAD_DIM
        decay = jnp.exp(jnp.concatenate([jnp.where(half, tot[0], tot[1]),
                                         jnp.where(half, tot[2], tot[3])], axis=1))
        st_ref[...] = st_old * decay + s_new
        if want_y:
            y = jnp.dot(jnp.concatenate(lhs_y, axis=1), jnp.concatenate(rhs_y, axis=0),
                        preferred_element_type=F32)
            y = y + xc * dskip_ref[...]
            gated = y * _silu(z_rows.astype(F32))
            ms = jnp.mean(gated * gated, axis=-1, keepdims=True)
            y_ref[0, pl.ds(out_row0, CHUNK), :] = (gated * lax.rsqrt(ms + RMS_EPS) * nw_ref[...]).astype(y_ref.dtype)

    lane_pos = lax.broadcasted_iota(jnp.int32, (1, CHUNK), 1)
    dt_meta = jnp.where(lane_pos >= PAD, _softplus(dtm_ref[0] + dt_bias), 0.0)
    chunk_step(U_LEAD, dt_meta, None, None)

    def body(s, carry):
        row0 = pl.multiple_of(s * CHUNK, CHUNK)
        dt8 = _softplus(dt_ref[0, :, pl.ds(row0, CHUNK)] + dt_bias)
        chunk_step(U_SEQ0 + row0, dt8, z_ref[0, pl.ds(row0, CHUNK), :], row0)
        return carry

    lax.fori_loop(0, n_chunks, body, 0, unroll=8)


def _ssd(proj3, proj_meta, dt3, dt3_meta, conv_w, conv_b2, dtb3, alog3, dskip_row, nw_row):
    nb, seq, _ = proj3.shape
    gx, gb, gc = COL_X // GROUP_W, COL_B // D_STATE, COL_C // D_STATE
    u_rows = U_SEQ0 + seq
    return pl.pallas_call(
        _ssd_kernel,
        grid=(nb, SSM_GROUPS),
        in_specs=[
            pl.BlockSpec((1, seq, GROUP_W), lambda b, g: (b, 0, gx + g)),
            pl.BlockSpec((1, seq, D_STATE), lambda b, g: (b, 0, gb + g)),
            pl.BlockSpec((1, seq, D_STATE), lambda b, g: (b, 0, gc + g)),
            pl.BlockSpec((1, seq, GROUP_W), lambda b, g: (b, 0, g)),
            pl.BlockSpec((CHUNK, GROUP_W), lambda b, g: (0, gx + g)),
            pl.BlockSpec((CHUNK, D_STATE), lambda b, g: (0, gb + g)),
            pl.BlockSpec((CHUNK, D_STATE), lambda b, g: (0, gc + g)),
            pl.BlockSpec((1, DT_ROWS, seq), lambda b, g: (g, 0, b)),
            pl.BlockSpec((1, DT_ROWS, CHUNK), lambda b, g: (g, 0, 0)),
            pl.BlockSpec((CONV_K, GROUP_W), lambda b, g: (0, g)),
            pl.BlockSpec((CONV_K, D_STATE), lambda b, g: (0, D_SSM // D_STATE + g)),
            pl.BlockSpec((CONV_K, D_STATE), lambda b, g: (0, D_SSM // D_STATE + SSM_GROUPS + g)),
            pl.BlockSpec((1, GROUP_W), lambda b, g: (0, g)),
            pl.BlockSpec((1, D_STATE), lambda b, g: (0, D_SSM // D_STATE + g)),
            pl.BlockSpec((1, D_STATE), lambda b, g: (0, D_SSM // D_STATE + SSM_GROUPS + g)),
            pl.BlockSpec((1, DT_ROWS, 1), lambda b, g: (g, 0, 0)),
            pl.BlockSpec((1, DT_ROWS, 1), lambda b, g: (g, 0, 0)),
            pl.BlockSpec((1, GROUP_W), lambda b, g: (0, g)),
            pl.BlockSpec((1, GROUP_W), lambda b, g: (0, g)),
        ],
        out_specs=pl.BlockSpec((1, seq, GROUP_W), lambda b, g: (b, 0, g)),
        out_shape=jax.ShapeDtypeStruct((nb, seq, D_SSM), BF16),
        scratch_shapes=[
            pltpu.VMEM((u_rows, GROUP_W), F32),
            pltpu.VMEM((u_rows, D_STATE), F32),
            pltpu.VMEM((u_rows, D_STATE), F32),
            pltpu.VMEM((D_STATE, GROUP_W), F32),
            pltpu.VMEM((CHUNK, CHUNK), BF16),
            pltpu.VMEM((CHUNK, CHUNK), F32),
            pltpu.VMEM((2 * CHUNK, 2 * CHUNK), BF16),
            pltpu.VMEM((SSM_HPG, CHUNK, GROUP_W), BF16),
        ],
        compiler_params=pltpu.CompilerParams(
            dimension_semantics=("arbitrary", "arbitrary"), vmem_limit_bytes=VMEM_LIMIT),
        name="ssd",
    )(proj3, proj3, proj3, proj3, proj_meta, proj_meta, proj_meta, dt3, dt3_meta,
      conv_w, conv_w, conv_w, conv_b2, conv_b2, conv_b2, dtb3, alog3, dskip_row, nw_row)


def _attn_kernel(slope_ref, q_ref, k_ref, v_ref, km_ref, vm_ref, lam_ref, nw_ref, wu_ref, wd_ref,
                 o_ref, wub_ref, wdb_ref, kaug_ref, vt_ref, s_ref, p_ref):
    wub_ref[...] = wu_ref[...].astype(BF16)
    wdb_ref[...] = wd_ref[...].astype(BF16)

    h = pl.program_id(1)
    seq = q_ref.shape[1]
    tq = ATT_TQ
    slope = slope_ref[h]

    sub8 = lax.broadcasted_iota(jnp.int32, (8, CHUNK), 0)
    lane8 = lax.broadcasted_iota(jnp.int32, (8, CHUNK), 1)

    def stage(r0, k, v, valid_from):
        pos = lane8 + r0
        bias = (slope * LOG2E) * pos.astype(F32)
        if valid_from:
            bias = jnp.where(pos >= valid_from, bias, NEG_INF)
        b1 = bias.astype(BF16).astype(F32)
        b2 = (bias - b1).astype(BF16).astype(F32)
        b3 = bias - b1 - b2
        rows8 = jnp.where(sub8 == 0, b1, jnp.where(sub8 == 1, b2, jnp.where(sub8 == 2, b3, 0.0)))
        cols = jnp.concatenate([rows8, jnp.zeros((CHUNK - 8, CHUNK), F32)], axis=0).T
        kaug_ref[r0:r0 + CHUNK, 0:128] = (k.astype(F32) * LOG2E).astype(BF16)
        kaug_ref[r0:r0 + CHUNK, 128:256] = cols.astype(BF16)
        vt_ref[0:ATT_V_DIM, r0:r0 + CHUNK] = v.T

    vt_ref[ATT_V_DIM:, :] = jnp.ones((vt_ref.shape[0] - ATT_V_DIM, vt_ref.shape[1]), BF16)
    stage(0, km_ref[...], vm_ref[...], PAD)
    for j in range(seq // CHUNK):
        rows = slice(CHUNK * j, CHUNK * (j + 1))
        stage(CHUNK * (j + 1), k_ref[0, rows, :], v_ref[0, rows, :], 0)

    lam4 = lam_ref[...]
    lam = (jnp.exp(jnp.sum(lam4[0:1] * lam4[1:2], axis=-1, keepdims=True))
           - jnp.exp(jnp.sum(lam4[2:3] * lam4[3:4], axis=-1, keepdims=True)) + LAMBDA_INIT)

    lane_q = lax.broadcasted_iota(jnp.int32, (tq, 128), 1)
    ones_blk = jnp.where(lane_q < 3, 1.0, 0.0).astype(BF16)
    map_lanes = (lane_q < ATT_QK_DIM, lane_q >= ATT_QK_DIM)
    hq = tq // 2
    tri = (lax.broadcasted_iota(jnp.int32, (hq, hq), 0)
           <= lax.broadcasted_iota(jnp.int32, (hq, hq), 1))

    def scores(rows, qa):
        return lax.dot_general(kaug_ref[rows, :], qa, CONTRACT_LAST, preferred_element_type=F32)

    class Item:
        def __init__(self, t, c):
            self.t, self.c = t, c
            d0 = CHUNK + tq * t
            self.kend = d0 + tq
            self.rows_a, self.rows_b = slice(d0, d0 + hq), slice(d0 + hq, self.kend)
            self.blocks = [slice(0, CHUNK)] + [slice(CHUNK + tq * j, CHUNK + tq * (j + 1)) for j in range(t)]
            self.m = None
            self.qaug = None

        def score_steps(self):
            c = self.c

            def first():
                q = q_ref[0, tq * self.t:tq * (self.t + 1), :] * jnp.asarray(ATT_QK_DIM ** -0.5, BF16)
                self.qaug = jnp.concatenate([jnp.where(map_lanes[c], q, jnp.zeros_like(q)), ones_blk], axis=1)

            def full(rows):
                def step():
                    sv = scores(rows, self.qaug)
                    s_ref[c, rows, :] = sv
                    bm = jnp.max(sv, axis=0, keepdims=True)
                    self.m = bm if self.m is None else jnp.maximum(self.m, bm)
                return step

            def diag_a():
                sv = scores(self.rows_a, self.qaug)
                sv = jnp.concatenate([jnp.where(tri, sv[:, :hq], NEG_INF), sv[:, hq:]], axis=1)
                s_ref[c, self.rows_a, :] = sv
                self.m = jnp.maximum(self.m, jnp.max(sv, axis=0, keepdims=True))

            def diag_b():
                sv = jnp.where(tri, scores(self.rows_b, self.qaug[hq:, :]), NEG_INF)
                s_ref[c, self.rows_b, hq:] = sv
                m = self.m
                self.m = jnp.concatenate(
                    [m[:, :hq], jnp.maximum(m[:, hq:], jnp.max(sv, axis=0, keepdims=True))], axis=1)

            return [first] + [full(r) for r in self.blocks] + [diag_a, diag_b]

        def prob_steps(self):
            c = self.c

            def full(rows):
                def step():
                    p_ref[c, rows, :] = jnp.exp2(s_ref[c, rows, :] - self.m).astype(BF16)
                return step

            def diag_b():
                p_ref[c, self.rows_b, :hq] = jnp.zeros((hq, hq), BF16)
                p_ref[c, self.rows_b, hq:] = jnp.exp2(s_ref[c, self.rows_b, hq:] - self.m[:, hq:]).astype(BF16)

            return [full(r) for r in self.blocks + [self.rows_a]] + [diag_b]

        def value_steps(self):
            self.acc = None

            def part(rows):
                def step():
                    d = jnp.dot(vt_ref[:, rows], p_ref[self.c, rows, :], preferred_element_type=F32)
                    self.acc = d if self.acc is None else self.acc + d
                return step

            return [part(r) for r in self.blocks + [slice(self.rows_a.start, self.kend)]]

        def value_out(self):
            return self.acc[:ATT_V_DIM] / self.acc[ATT_V_DIM:ATT_V_DIM + 1]

    items = [Item(t, c) for t in range(seq // tq) for c in range(2)]
    n_items = len(items)
    outs = {}

    def finish(i):
        item = items[i]
        outs[i] = item.value_out()
        if item.c == 1:
            o = outs.pop(i - 1) - lam * outs.pop(i)
            ms = jnp.mean(o * o, axis=0, keepdims=True)
            o = o * lax.rsqrt(ms + RMS_EPS) * nw_ref[...] * (1.0 - LAMBDA_INIT)
            o_ref[0, tq * item.t:tq * (item.t + 1), :] = o.T.astype(o_ref.dtype)

    for stage in range(-2, n_items):
        lanes = []
        if 0 <= stage + 2 < n_items:
            lanes.append(items[stage + 2].score_steps())
        if 0 <= stage + 1 < n_items:
            lanes.append(items[stage + 1].prob_steps())
        if 0 <= stage:
            lanes.append(items[stage].value_steps() + [functools.partial(finish, stage)])
        for k in range(max(len(steps) for steps in lanes)):
            for steps in lanes:
                if k < len(steps):
                    steps[k]()


def _attention(slopes, proj3, proj_meta, lam4, nw_col, w_up, w_down):
    nb, seq, _ = proj3.shape
    steps = nb * ATT_HEADS
    wu3 = w_up.reshape(steps, D_MODEL // steps, D_FF)
    wd3 = w_down.reshape(steps, D_FF // steps, D_MODEL)
    w_blk = lambda a: pl.BlockSpec((1,) + a.shape[1:], lambda b, h: (b * ATT_HEADS + h, 0, 0))
    cq, ck, cv = COL_Q // ATT_V_DIM, COL_K // ATT_V_DIM, COL_V // ATT_V_DIM
    kv_rows = CHUNK + seq
    seq_blk = lambda c0: pl.BlockSpec((1, seq, ATT_V_DIM), lambda b, h: (b, 0, c0 + h))
    meta_blk = lambda c0: pl.BlockSpec((CHUNK, ATT_V_DIM), lambda b, h: (0, c0 + h))
    return pl.pallas_call(
        _attn_kernel,
        grid=(nb, ATT_HEADS),
        in_specs=[
            pl.BlockSpec(memory_space=pltpu.SMEM),
            seq_blk(cq), seq_blk(ck), seq_blk(cv), meta_blk(ck), meta_blk(cv),
            pl.BlockSpec((4, ATT_QK_DIM), lambda b, h: (0, 0)),
            pl.BlockSpec((ATT_V_DIM, 1), lambda b, h: (0, 0)),
            w_blk(wu3), w_blk(wd3),
        ],
        out_specs=[pl.BlockSpec((1, seq, ATT_V_DIM), lambda b, h: (b, 0, h)), w_blk(wu3), w_blk(wd3)],
        out_shape=[jax.ShapeDtypeStruct((nb, seq, D_ATT), BF16),
                   jax.ShapeDtypeStruct(wu3.shape, BF16), jax.ShapeDtypeStruct(wd3.shape, BF16)],
        scratch_shapes=[
            pltpu.VMEM((kv_rows, 2 * ATT_V_DIM), BF16),
            pltpu.VMEM((ATT_V_DIM + 16, kv_rows), BF16),
            pltpu.VMEM((2, kv_rows, ATT_TQ), F32),
            pltpu.VMEM((2, kv_rows, ATT_TQ), BF16),
        ],
        compiler_params=pltpu.CompilerParams(
            dimension_semantics=("arbitrary", "arbitrary"), vmem_limit_bytes=VMEM_LIMIT),
        name="diffattn",
    )(slopes, proj3, proj3, proj3, proj_meta, proj_meta, lam4, nw_col, wu3, wd3)


def _outproj_kernel(y_ref, o_ref, wy_ref, wo_ref, mix_ref, wyb_ref, wob_ref):
    @pl.when(pl.program_id(1) == 0)
    def _():
        wyb_ref[...] = wy_ref[...].astype(BF16)
        wob_ref[...] = wo_ref[...].astype(BF16)

    mix_ref[...] = (jnp.dot(y_ref[...], wyb_ref[...], preferred_element_type=F32)
                    + jnp.dot(o_ref[...], wob_ref[...], preferred_element_type=F32))


def _outproj(y2d, o2d, w_out, *, tm, tn):
    m = y2d.shape[0]
    return pl.pallas_call(
        _outproj_kernel,
        grid=(D_MODEL // tn, m // tm),
        in_specs=[
            pl.BlockSpec((tm, D_SSM), lambda j, i: (i, 0)),
            pl.BlockSpec((tm, D_ATT), lambda j, i: (i, 0)),
            pl.BlockSpec((D_SSM, tn), lambda j, i: (0, j)),
            pl.BlockSpec((D_ATT, tn), lambda j, i: (1, j)),
        ],
        out_specs=pl.BlockSpec((tm, tn), lambda j, i: (i, j)),
        out_shape=jax.ShapeDtypeStruct((m, D_MODEL), F32),
        scratch_shapes=[pltpu.VMEM((D_SSM, tn), BF16), pltpu.VMEM((D_ATT, tn), BF16)],
        compiler_params=pltpu.CompilerParams(
            dimension_semantics=("arbitrary", "arbitrary"), vmem_limit_bytes=VMEM_LIMIT),
        name="outproj",
    )(y2d, o2d, w_out, w_out)


def _mlp_kernel(x_ref, mix_ref, wu_ref, wd_ref, g0_ref, b0_ref, g1_ref, b1_ref, g2_ref, b2_ref,
                o_ref, h1_ref, hb_ref, *, ln_rows):
    f = pl.program_id(1)
    n_ln = x_ref.shape[0] // ln_rows

    @pl.when(f == 0)
    def _():
        def body(r, carry):
            rows = pl.ds(pl.multiple_of(r * ln_rows, ln_rows), ln_rows)
            h0 = _layer_norm_rows(x_ref[rows, :], g0_ref[...], b0_ref[...])
            h1 = _layer_norm_rows(ALPHA * h0 + mix_ref[rows, :], g1_ref[...], b1_ref[...])
            h1_ref[rows, :] = h1
            hb_ref[rows, :] = h1.astype(BF16)
            return carry

        lax.fori_loop(0, n_ln, body, 0)
        o_ref[...] = jnp.zeros(o_ref.shape, F32)

    u = jnp.dot(hb_ref[...], wu_ref[...], preferred_element_type=F32)
    u = jnp.square(jnp.maximum(u, 0.0)).astype(BF16)
    for c0 in range(0, D_MODEL, MLP_ACC_SLAB):
        cols = slice(c0, c0 + MLP_ACC_SLAB)
        o_ref[:, cols] += jnp.dot(u, wd_ref[:, cols], preferred_element_type=F32)

    @pl.when(f == pl.num_programs(1) - 1)
    def _():
        def body(r, carry):
            rows = pl.ds(pl.multiple_of(r * ln_rows, ln_rows), ln_rows)
            o_ref[rows, :] = _layer_norm_rows(ALPHA * h1_ref[rows, :] + o_ref[rows, :], g2_ref[...], b2_ref[...])
            return carry

        lax.fori_loop(0, n_ln, body, 0)


def _mlp(x2d, mix, w_up, w_down, g0, b0, g1, b1, g2, b2, *, tm, tf):
    m = x2d.shape[0]
    row = pl.BlockSpec((1, D_MODEL), lambda i, f: (0, 0))
    return pl.pallas_call(
        functools.partial(_mlp_kernel, ln_rows=128),
        grid=(m // tm, D_FF // tf),
        in_specs=[
            pl.BlockSpec((tm, D_MODEL), lambda i, f: (i, 0)),
            pl.BlockSpec((tm, D_MODEL), lambda i, f: (i, 0)),
            pl.BlockSpec((D_MODEL, tf), lambda i, f: (0, f)),
            pl.BlockSpec((tf, D_MODEL), lambda i, f: (f, 0)),
            row, row, row, row, row, row,
        ],
        out_specs=pl.BlockSpec((tm, D_MODEL), lambda i, f: (i, 0)),
        out_shape=jax.ShapeDtypeStruct((m, D_MODEL), F32),
        scratch_shapes=[pltpu.VMEM((tm, D_MODEL), F32), pltpu.VMEM((tm, D_MODEL), BF16)],
        compiler_params=pltpu.CompilerParams(
            dimension_semantics=("arbitrary", "arbitrary"), vmem_limit_bytes=VMEM_LIMIT),
        name="mlp",
    )(x2d, mix, w_up, w_down, g0, b0, g1, b1, g2, b2)


def kernel(x, meta_tokens, ln0_g, ln0_b, w_in, conv_w, conv_b, dt_bias, a_log, d_skip, ssd_norm_w,
           lambda_q1, lambda_k1, lambda_q2, lambda_k2, attn_norm_w, w_out, ln1_g, ln1_b, w_up, w_down,
           ln2_g, ln2_b):
    nb, seq, d = x.shape
    assert (d, w_in.shape[0]) == (D_MODEL, DEPTH) and seq % ATT_TQ == 0
    m = nb * seq
    x2d = x.reshape(m, d)
    row = lambda v: v.reshape(1, -1).astype(F32)

    w_in_t = w_in[0].astype(F32).T
    wdt = w_in_t[W_IN_DT:W_IN_DT + SSM_HEADS].reshape(SSM_GROUPS, SSM_HPG, d)
    wdt = jnp.pad(wdt, ((0, 0), (0, DT_ROWS - SSM_HPG), (0, 0))).reshape(SSM_GROUPS * DT_ROWS, d).astype(BF16)
    pad_heads = lambda v: jnp.pad(v.reshape(SSM_GROUPS, SSM_HPG).astype(F32),
                                  ((0, 0), (0, DT_ROWS - SSM_HPG)))[..., None]
    dtb3, alog3 = pad_heads(dt_bias[0]), pad_heads(a_log[0])
    dskip_row = row(jnp.repeat(d_skip[0], SSM_HEAD_DIM))
    lam4 = jnp.stack([lambda_q1[0], lambda_k1[0], lambda_q2[0], lambda_k2[0]]).astype(F32)
    slopes = jnp.asarray(2.0 ** (-8.0 * np.arange(1, ATT_HEADS + 1) / ATT_HEADS), dtype=F32)
    g0, b0 = row(ln0_g), row(ln0_b)
    meta_pad = jnp.pad(meta_tokens.astype(F32), ((PAD, 0), (0, 0)))

    hn, dt_t, hm, dt_t_meta = _ln0(x2d, meta_pad, g0, b0, wdt, tm=1024)
    proj, proj_meta = _inproj(hn, hm, w_in_t.reshape(-1, SSM_HEADS, d), tm=2048, tn=768)
    proj3 = proj.reshape(nb, seq, N_PROJ)
    dt3 = dt_t.reshape(SSM_GROUPS, DT_ROWS, m)
    dt3_meta = dt_t_meta.reshape(SSM_GROUPS, DT_ROWS, CHUNK)

    y = _ssd(proj3, proj_meta, dt3, dt3_meta, conv_w[0].astype(F32), row(conv_b[0]),
             dtb3, alog3, dskip_row, row(ssd_norm_w[0]))
    o, w_up_b, w_down_b = _attention(slopes, proj3, proj_meta, lam4,
                                     attn_norm_w[0].reshape(ATT_V_DIM, 1).astype(F32),
                                     w_up[0].astype(F32), w_down[0].astype(F32))

    mix = _outproj(y.reshape(m, D_SSM), o.reshape(m, D_ATT), w_out[0].astype(F32), tm=1024, tn=512)
    h2 = _mlp(x2d, mix, w_up_b.reshape(D_MODEL, D_FF), w_down_b.reshape(D_FF, D_MODEL),
              g0, b0, row(ln1_g[0]), row(ln1_b[0]), row(ln2_g[0]), row(ln2_b[0]), tm=512, tf=1024)
    return h2.reshape(nb, seq, d)
```

```python
import functools
import math

import jax
import jax.numpy as jnp
import numpy as np
from jax import lax
from jax.experimental import pallas as pl
from jax.experimental.pallas import tpu as pltpu

F32 = jnp.float32
BF16 = jnp.bfloat16

D_MODEL = 2048
N_META = 16
CHUNK = 128
PAD = CHUNK - N_META
D_SSM = 2048
D_ATT = 2048
SSM_HEAD_DIM = 64
SSM_HEADS = 32
SSM_GROUPS = 8
SSM_HPG = 4
GROUP_W = SSM_HPG * SSM_HEAD_DIM
D_STATE = 128
CONV_K = 4
D_CONV = D_SSM + 2 * SSM_GROUPS * D_STATE
ATT_V_DIM = 128
ATT_HEADS = 16
ATT_QK_DIM = 64
D_FF = 4 * D_MODEL
DEPTH = 1
ALPHA = (2 * DEPTH) ** 0.25
LN_EPS = 1e-5
RMS_EPS = 1e-5
NEG_INF = -1e30
LOG2E = math.log2(math.e)
LAMBDA_INIT = 0.8 - 0.6 * math.exp(-0.3 * 0)

COL_Z = 0
COL_X = D_SSM
COL_B = COL_X + D_SSM
COL_C = COL_B + SSM_GROUPS * D_STATE
W_IN_DT = D_SSM + D_CONV
W_IN_Q = W_IN_DT + SSM_HEADS
N_QKV = 3 * D_ATT
COL_Q = 0
COL_K = D_ATT
COL_V = 2 * D_ATT
DT_ROWS = 8

V7X_VMEM_BYTES = 64 * 1024 * 1024
VMEM_LIMIT = 56 * 1024 * 1024

ATT_TQ = 512
MLP_ACC_SLAB = 512


def _layer_norm_rows(x, g, b):
    mu = jnp.mean(x, axis=-1, keepdims=True)
    xc = x - mu
    var = jnp.mean(xc * xc, axis=-1, keepdims=True)
    return xc * lax.rsqrt(var + LN_EPS) * g + b


def _silu(x):
    hx = 0.5 * x
    return hx + hx * jnp.tanh(hx)


def _softplus(x):
    return jnp.maximum(x, 0.0) + jnp.log1p(jnp.exp(-jnp.abs(x)))


CONTRACT_LAST = (((1,), (1,)), ((), ()))


def _ln0_kernel(x_ref, xm_ref, g_ref, b_ref, wdt_ref, hn_ref, dt_ref, hm_ref, dtm_ref, *, ln_rows):
    def body(r, carry):
        r0 = pl.multiple_of(r * ln_rows, ln_rows)
        hn = _layer_norm_rows(x_ref[pl.ds(r0, ln_rows), :], g_ref[...], b_ref[...])
        hn_ref[pl.ds(r0, ln_rows), :] = hn.astype(BF16)
        return carry

    lax.fori_loop(0, x_ref.shape[0] // ln_rows, body, 0)
    dt_ref[...] = lax.dot_general(wdt_ref[...], hn_ref[...], CONTRACT_LAST, preferred_element_type=F32)

    @pl.when(pl.program_id(0) == 0)
    def _():
        hm_ref[...] = _layer_norm_rows(xm_ref[...], g_ref[...], b_ref[...]).astype(BF16)
        dtm_ref[...] = lax.dot_general(wdt_ref[...], hm_ref[...], CONTRACT_LAST, preferred_element_type=F32)


def _ln0(x2d, meta_pad, g, b, wdt, *, tm):
    m = x2d.shape[0]
    ndt = wdt.shape[0]
    const = lambda i: (0, 0)
    return pl.pallas_call(
        functools.partial(_ln0_kernel, ln_rows=128),
        grid=(m // tm,),
        in_specs=[
            pl.BlockSpec((tm, D_MODEL), lambda i: (i, 0)),
            pl.BlockSpec((CHUNK, D_MODEL), const),
            pl.BlockSpec((1, D_MODEL), const),
            pl.BlockSpec((1, D_MODEL), const),
            pl.BlockSpec((ndt, D_MODEL), const),
        ],
        out_specs=[
            pl.BlockSpec((tm, D_MODEL), lambda i: (i, 0)),
            pl.BlockSpec((ndt, tm), lambda i: (0, i)),
            pl.BlockSpec((CHUNK, D_MODEL), const),
            pl.BlockSpec((ndt, CHUNK), const),
        ],
        out_shape=[
            jax.ShapeDtypeStruct((m, D_MODEL), BF16),
            jax.ShapeDtypeStruct((ndt, m), F32),
            jax.ShapeDtypeStruct((CHUNK, D_MODEL), BF16),
            jax.ShapeDtypeStruct((ndt, CHUNK), F32),
        ],
        compiler_params=pltpu.CompilerParams(dimension_semantics=("arbitrary",), vmem_limit_bytes=VMEM_LIMIT),
        name="ln0",
    )(x2d, meta_pad, g, b, wdt)


def _inproj_kernel(hn_ref, hm_ref, wt_ref, o_ref, om_ref, wb_ref):
    j = pl.program_id(0)
    tn = wb_ref.shape[0]
    new_tile = pl.program_id(1) == 0
    unused = j < COL_K // tn

    @pl.when(new_tile)
    def _():
        wb_ref[...] = wt_ref[...].reshape(wb_ref.shape).astype(BF16)

    @pl.when(new_tile & jnp.logical_not(unused))
    def _():
        om_ref[...] = lax.dot_general(hm_ref[...], wb_ref[...], CONTRACT_LAST,
                                      preferred_element_type=F32).astype(om_ref.dtype)

    @pl.when(new_tile & unused)
    def _():
        om_ref[...] = jnp.zeros(om_ref.shape, om_ref.dtype)

    o_ref[...] = lax.dot_general(hn_ref[...], wb_ref[...], CONTRACT_LAST,
                                 preferred_element_type=F32).astype(o_ref.dtype)


def _inproj(hn, hm, wt, *, tm, tn):
    m = hn.shape[0]
    n = N_QKV
    assert n % tn == 0 and tn % SSM_HEADS == 0 and wt.shape[1] == SSM_HEADS and W_IN_Q % SSM_HEADS == 0
    groups = tn // SSM_HEADS
    group_start = lambda j: W_IN_Q // SSM_HEADS + j * groups
    return pl.pallas_call(
        _inproj_kernel,
        grid=(n // tn, m // tm),
        in_specs=[
            pl.BlockSpec((tm, D_MODEL), lambda j, i: (i, 0)),
            pl.BlockSpec((CHUNK, D_MODEL), lambda j, i: (0, 0)),
            pl.BlockSpec((pl.Element(groups), pl.Element(SSM_HEADS), pl.Element(D_MODEL)),
                         lambda j, i: (group_start(j), 0, 0)),
        ],
        out_specs=[
            pl.BlockSpec((tm, tn), lambda j, i: (i, j)),
            pl.BlockSpec((CHUNK, tn), lambda j, i: (0, j)),
        ],
        out_shape=[
            jax.ShapeDtypeStruct((m, n), BF16),
            jax.ShapeDtypeStruct((CHUNK, n), BF16),
        ],
        scratch_shapes=[pltpu.VMEM((tn, D_MODEL), BF16)],
        compiler_params=pltpu.CompilerParams(
            dimension_semantics=("arbitrary", "arbitrary"), vmem_limit_bytes=VMEM_LIMIT),
        name="inproj",
    )(hn, hm, wt)


U_LEAD = 16
U_SEQ0 = U_LEAD + CHUNK


def _conv_silu(u_ref, r0, w_ref, b_ref):
    win = u_ref[pl.ds(r0 - 8, CHUNK + 8), :]
    w = w_ref[...]
    acc = b_ref[...] + w[3:4, :] * win[8:CHUNK + 8, :]
    acc = acc + w[2:3, :] * win[7:CHUNK + 7, :]
    acc = acc + w[1:2, :] * win[6:CHUNK + 6, :]
    acc = acc + w[0:1, :] * win[5:CHUNK + 5, :]
    return _silu(acc)


SSD_PROJ_ROWS = 512


def _ssd_kernel(hn_ref, hmeta_ref, wz_ref, wx_ref, wb_ref, wc_ref, dt_ref, dtm_ref,
                cwx_ref, cwb_ref, cwc_ref, cbx_ref, cbb_ref, cbc_ref,
                dtb_ref, alog_ref, dskip_ref, nw_ref,
                y_ref, w_ref, z_ref, ux_ref, ub_ref, uc_ref, st_ref, m1_ref, tril_ref, m2_ref, hm_ref):
    seq = hn_ref.shape[0]

    li = lax.broadcasted_iota(jnp.int32, (CHUNK, CHUNK), 0)
    ki = lax.broadcasted_iota(jnp.int32, (CHUNK, CHUNK), 1)
    lower = jnp.where(ki <= li, 1.0, 0.0)
    tril_ref[...] = lower
    m1_ref[...] = lower.astype(BF16)
    k2 = lax.broadcasted_iota(jnp.int32, (2 * CHUNK, 2 * CHUNK), 0) & (CHUNK - 1)
    s2 = lax.broadcasted_iota(jnp.int32, (2 * CHUNK, 2 * CHUNK), 1)
    m2_ref[...] = jnp.where((s2 >= CHUNK) | (k2 > s2), 1.0, 0.0).astype(BF16)
    head_of = lax.broadcasted_iota(jnp.int32, (CHUNK, GROUP_W), 1) >> 6
    for r in range(SSM_HPG):
        hm_ref[r] = jnp.where(head_of == r, 1.0, 0.0).astype(BF16)

    nz, nx, nbc = GROUP_W, GROUP_W, D_STATE
    w_ref[0:nz, :] = wz_ref[...].reshape(nz, D_MODEL).astype(BF16)
    w_ref[nz:nz + nx, :] = wx_ref[...].reshape(nx, D_MODEL).astype(BF16)
    w_ref[nz + nx:nz + nx + nbc, :] = wb_ref[...].reshape(nbc, D_MODEL).astype(BF16)
    w_ref[nz + nx + nbc:, :] = wc_ref[...].reshape(nbc, D_MODEL).astype(BF16)

    for u_ref in (ux_ref, ub_ref, uc_ref):
        u_ref[0:U_LEAD, :] = jnp.zeros((U_LEAD, u_ref.shape[1]), F32)
    st_ref[...] = jnp.zeros(st_ref.shape, F32)
    meta_valid = lax.broadcasted_iota(jnp.int32, (CHUNK, 1), 0) >= PAD

    def project(lhs, u_row0, z_row0, meta_rows):
        res = lax.dot_general(lhs, w_ref[...], CONTRACT_LAST, preferred_element_type=F32)
        if meta_rows:
            head = jnp.where(meta_valid, res[:meta_rows], 0.0)
            res_u = jnp.concatenate([head, res[meta_rows:]], axis=0)
        else:
            res_u = res
        n = res.shape[0]
        ux_ref[u_row0:u_row0 + n, :] = res_u[:, nz:nz + nx]
        ub_ref[u_row0:u_row0 + n, :] = res_u[:, nz + nx:nz + nx + nbc]
        uc_ref[u_row0:u_row0 + n, :] = res_u[:, nz + nx + nbc:]
        z_ref[z_row0:z_row0 + n - meta_rows, :] = res[meta_rows:, 0:nz]

    def project_block(blk):
        r0 = blk * SSD_PROJ_ROWS
        if blk == 0:
            lhs = jnp.concatenate([hmeta_ref[...], hn_ref[0:SSD_PROJ_ROWS, :]], axis=0)
            project(lhs, U_LEAD, 0, CHUNK)
        else:
            project(hn_ref[r0:r0 + SSD_PROJ_ROWS, :], U_SEQ0 + r0, r0, 0)

    dt_bias = dtb_ref[0]
    a_neg = -jnp.exp(alog_ref[0])

    def chunk_step(r0, dt8, z_rows, out_row0):
        xc = _conv_silu(ux_ref, r0, cwx_ref, cbx_ref)
        bc = _conv_silu(ub_ref, r0, cwb_ref, cbb_ref)
        a8 = dt8 * a_neg
        xs_bf = xc.astype(BF16)
        bt = bc.T
        st_old = st_ref[...]
        want_y = z_rows is not None
        if want_y:
            cc = _conv_silu(uc_ref, r0, cwc_ref, cbc_ref)
            cb = lax.dot_general(cc.astype(BF16), bc.astype(BF16), CONTRACT_LAST,
                                 preferred_element_type=F32)
            cb_low = cb * tril_ref[...]
            st_bf = st_old.astype(BF16)
        lhs_y, rhs_y, lhs_s, rhs_s, tot = [], [], [], [], []
        for r in range(SSM_HPG):
            a_r = a8[r:r + 1, :]
            dt_r = dt8[r:r + 1, :]
            a_hi = a_r.astype(BF16)
            a_lo = (a_r - a_hi.astype(F32)).astype(BF16)
            m1 = m1_ref[...]
            dd = jnp.dot(jnp.concatenate([m1 * a_hi, m1 * a_lo], axis=1), m2_ref[...],
                         preferred_element_type=F32)
            dseg = dd[:, :CHUNK]
            acsb = dd[:, CHUNK:]
            lmat = jnp.exp(dseg)
            xs_r = xs_bf * hm_ref[r]
            if want_y:
                lhs_y.append((cb_low * lmat * dt_r).astype(BF16))
                lhs_y.append((cc * jnp.exp(acsb)).astype(BF16))
                rhs_y.append(xs_r)
                rhs_y.append(st_bf * hm_ref[r])
            f1 = lmat[CHUNK - 1:CHUNK, :] * dt_r
            lhs_s.append((bt * f1).astype(BF16))
            rhs_s.append(xs_r)
            tot.append(acsb[CHUNK - 1:CHUNK, :])
        s_new = jnp.dot(jnp.concatenate(lhs_s, axis=1), jnp.concatenate(rhs_s, axis=0),
                        preferred_element_type=F32)
        half = lax.broadcasted_iota(jnp.int32, (1, CHUNK), 1) < SSM_HEAD_DIM
        decay = jnp.exp(jnp.concatenate([jnp.where(half, tot[0], tot[1]),
                                         jnp.where(half, tot[2], tot[3])], axis=1))
        st_ref[...] = st_old * decay + s_new
        if want_y:
            y = jnp.dot(jnp.concatenate(lhs_y, axis=1), jnp.concatenate(rhs_y, axis=0),
                        preferred_element_type=F32)
            y = y + xc * dskip_ref[...]
            gated = y * _silu(z_rows.astype(F32))
            ms = jnp.mean(gated * gated, axis=-1, keepdims=True)
            y_ref[0, out_row0:out_row0 + CHUNK, :] = (gated * lax.rsqrt(ms + RMS_EPS) * nw_ref[...]).astype(y_ref.dtype)

    def meta_chunk():
        lane_pos = lax.broadcasted_iota(jnp.int32, (1, CHUNK), 1)
        dt_meta = jnp.where(lane_pos >= PAD, _softplus(dtm_ref[0] + dt_bias), 0.0)
        chunk_step(U_LEAD, dt_meta, None, None)

    def seq_chunk(s):
        row0 = s * CHUNK
        dt8 = _softplus(dt_ref[0, :, row0:row0 + CHUNK] + dt_bias)
        chunk_step(U_SEQ0 + row0, dt8, z_ref[row0:row0 + CHUNK, :], row0)

    per_blk = SSD_PROJ_ROWS // CHUNK
    n_blk = seq // SSD_PROJ_ROWS
    project_block(0)
    for blk in range(n_blk):
        if blk + 1 < n_blk:
            project_block(blk + 1)
        if blk == 0:
            meta_chunk()
        for s in range(blk * per_blk, (blk + 1) * per_blk):
            seq_chunk(s)


def _ssd(hn, hm, wt3, dt3, dt3_meta, conv_w, conv_b2, dtb3, alog3, dskip_row, nw_row, *, nb):
    m = hn.shape[0]
    seq = m // nb
    u_rows = U_SEQ0 + seq
    grp = lambda col0, width: (lambda b, g: ((col0 + g * width) // SSM_HEADS, 0, 0))
    w_blk = lambda col0, width: pl.BlockSpec(
        (pl.Element(width // SSM_HEADS), pl.Element(SSM_HEADS), pl.Element(D_MODEL)), grp(col0, width))
    return pl.pallas_call(
        _ssd_kernel,
        grid=(nb, SSM_GROUPS),
        in_specs=[
            pl.BlockSpec((seq, D_MODEL), lambda b, g: (b, 0)),
            pl.BlockSpec((CHUNK, D_MODEL), lambda b, g: (0, 0)),
            w_blk(COL_Z, GROUP_W), w_blk(COL_X, GROUP_W), w_blk(COL_B, D_STATE), w_blk(COL_C, D_STATE),
            pl.BlockSpec((1, DT_ROWS, seq), lambda b, g: (g, 0, b)),
            pl.BlockSpec((1, DT_ROWS, CHUNK), lambda b, g: (g, 0, 0)),
            pl.BlockSpec((CONV_K, GROUP_W), lambda b, g: (0, g)),
            pl.BlockSpec((CONV_K, D_STATE), lambda b, g: (0, D_SSM // D_STATE + g)),
            pl.BlockSpec((CONV_K, D_STATE), lambda b, g: (0, D_SSM // D_STATE + SSM_GROUPS + g)),
            pl.BlockSpec((1, GROUP_W), lambda b, g: (0, g)),
            pl.BlockSpec((1, D_STATE), lambda b, g: (0, D_SSM // D_STATE + g)),
            pl.BlockSpec((1, D_STATE), lambda b, g: (0, D_SSM // D_STATE + SSM_GROUPS + g)),
            pl.BlockSpec((1, DT_ROWS, 1), lambda b, g: (g, 0, 0)),
            pl.BlockSpec((1, DT_ROWS, 1), lambda b, g: (g, 0, 0)),
            pl.BlockSpec((1, GROUP_W), lambda b, g: (0, g)),
            pl.BlockSpec((1, GROUP_W), lambda b, g: (0, g)),
        ],
        out_specs=pl.BlockSpec((1, seq, GROUP_W), lambda b, g: (b, 0, g)),
        out_shape=jax.ShapeDtypeStruct((nb, seq, D_SSM), BF16),
        scratch_shapes=[
            pltpu.VMEM((2 * GROUP_W + 2 * D_STATE, D_MODEL), BF16),
            pltpu.VMEM((seq, GROUP_W), F32),
            pltpu.VMEM((u_rows, GROUP_W), F32),
            pltpu.VMEM((u_rows, D_STATE), F32),
            pltpu.VMEM((u_rows, D_STATE), F32),
            pltpu.VMEM((D_STATE, GROUP_W), F32),
            pltpu.VMEM((CHUNK, CHUNK), BF16),
            pltpu.VMEM((CHUNK, CHUNK), F32),
            pltpu.VMEM((2 * CHUNK, 2 * CHUNK), BF16),
            pltpu.VMEM((SSM_HPG, CHUNK, GROUP_W), BF16),
        ],
        compiler_params=pltpu.CompilerParams(
            dimension_semantics=("arbitrary", "arbitrary"), vmem_limit_bytes=VMEM_LIMIT),
        name="ssd",
    )(hn, hm, wt3, wt3, wt3, wt3, dt3, dt3_meta,
      conv_w, conv_w, conv_w, conv_b2, conv_b2, conv_b2, dtb3, alog3, dskip_row, nw_row)


def _attn_kernel(slope_ref, q_ref, k_ref, v_ref, km_ref, vm_ref, lam_ref, nw_ref, wu_ref, wd_ref,
                 o_ref, wub_ref, wdb_ref, kaug_ref, vt_ref, s_ref, p_ref):
    wub_ref[...] = wu_ref[...].astype(BF16)
    wdb_ref[...] = wd_ref[...].astype(BF16)

    h = pl.program_id(1)
    seq = q_ref.shape[1]
    tq = ATT_TQ
    slope = slope_ref[h]

    sub8 = lax.broadcasted_iota(jnp.int32, (8, CHUNK), 0)
    lane8 = lax.broadcasted_iota(jnp.int32, (8, CHUNK), 1)

    def stage(r0, k, v, valid_from):
        pos = lane8 + r0
        bias = (slope * LOG2E) * pos.astype(F32)
        if valid_from:
            bias = jnp.where(pos >= valid_from, bias, NEG_INF)
        b1 = bias.astype(BF16).astype(F32)
        b2 = (bias - b1).astype(BF16).astype(F32)
        b3 = bias - b1 - b2
        rows8 = jnp.where(sub8 == 0, b1, jnp.where(sub8 == 1, b2, jnp.where(sub8 == 2, b3, 0.0)))
        cols = jnp.concatenate([rows8, jnp.zeros((CHUNK - 8, CHUNK), F32)], axis=0).T
        kaug_ref[r0:r0 + CHUNK, 0:128] = (k.astype(F32) * LOG2E).astype(BF16)
        kaug_ref[r0:r0 + CHUNK, 128:256] = cols.astype(BF16)
        vt_ref[0:ATT_V_DIM, r0:r0 + CHUNK] = v.T

    vt_ref[ATT_V_DIM:, :] = jnp.ones((vt_ref.shape[0] - ATT_V_DIM, vt_ref.shape[1]), BF16)
    stage(0, km_ref[...], vm_ref[...], PAD)
    for j in range(seq // CHUNK):
        rows = slice(CHUNK * j, CHUNK * (j + 1))
        stage(CHUNK * (j + 1), k_ref[0, rows, :], v_ref[0, rows, :], 0)

    lam4 = lam_ref[...]
    lam = (jnp.exp(jnp.sum(lam4[0:1] * lam4[1:2], axis=-1, keepdims=True))
           - jnp.exp(jnp.sum(lam4[2:3] * lam4[3:4], axis=-1, keepdims=True)) + LAMBDA_INIT)

    lane_q = lax.broadcasted_iota(jnp.int32, (tq, 128), 1)
    ones_blk = jnp.where(lane_q < 3, 1.0, 0.0).astype(BF16)
    map_lanes = (lane_q < ATT_QK_DIM, lane_q >= ATT_QK_DIM)
    hq = tq // 2
    tri = (lax.broadcasted_iota(jnp.int32, (hq, hq), 0)
           <= lax.broadcasted_iota(jnp.int32, (hq, hq), 1))

    def scores(rows, qa):
        return lax.dot_general(kaug_ref[rows, :], qa, CONTRACT_LAST, preferred_element_type=F32)

    class Item:
        def __init__(self, t, c):
            self.t, self.c = t, c
            d0 = CHUNK + tq * t
            self.kend = d0 + tq
            self.rows_a, self.rows_b = slice(d0, d0 + hq), slice(d0 + hq, self.kend)
            self.blocks = [slice(0, CHUNK)] + [slice(CHUNK + tq * j, CHUNK + tq * (j + 1)) for j in range(t)]
            self.m = None
            self.qaug = None

        def score_steps(self):
            c = self.c

            def first():
                q = q_ref[0, tq * self.t:tq * (self.t + 1), :] * jnp.asarray(ATT_QK_DIM ** -0.5, BF16)
                self.qaug = jnp.concatenate([jnp.where(map_lanes[c], q, jnp.zeros_like(q)), ones_blk], axis=1)

            def full(rows):
                def step():
                    sv = scores(rows, self.qaug)
                    s_ref[c, rows, :] = sv
                    bm = jnp.max(sv, axis=0, keepdims=True)
                    self.m = bm if self.m is None else jnp.maximum(self.m, bm)
                return step

            def diag_a():
                sv = scores(self.rows_a, self.qaug)
                sv = jnp.concatenate([jnp.where(tri, sv[:, :hq], NEG_INF), sv[:, hq:]], axis=1)
                s_ref[c, self.rows_a, :] = sv
                self.m = jnp.maximum(self.m, jnp.max(sv, axis=0, keepdims=True))

            def diag_b():
                sv = jnp.where(tri, scores(self.rows_b, self.qaug[hq:, :]), NEG_INF)
                s_ref[c, self.rows_b, hq:] = sv
                m = self.m
                self.m = jnp.concatenate(
                    [m[:, :hq], jnp.maximum(m[:, hq:], jnp.max(sv, axis=0, keepdims=True))], axis=1)

            return [first] + [full(r) for r in self.blocks] + [diag_a, diag_b]

        def prob_steps(self):
            c = self.c

            def full(rows):
                def step():
                    p_ref[c, rows, :] = jnp.exp2(s_ref[c, rows, :] - self.m).astype(BF16)
                return step

            def diag_b():
                p_ref[c, self.rows_b, :hq] = jnp.zeros((hq, hq), BF16)
                p_ref[c, self.rows_b, hq:] = jnp.exp2(s_ref[c, self.rows_b, hq:] - self.m[:, hq:]).astype(BF16)

            return [full(r) for r in self.blocks + [self.rows_a]] + [diag_b]

        def value_steps(self):
            self.acc = None

            def part(rows):
                def step():
                    d = jnp.dot(vt_ref[:, rows], p_ref[self.c, rows, :], preferred_element_type=F32)
                    self.acc = d if self.acc is None else self.acc + d
                return step

            return [part(r) for r in self.blocks + [slice(self.rows_a.start, self.kend)]]

        def value_out(self):
            return self.acc[:ATT_V_DIM] / self.acc[ATT_V_DIM:ATT_V_DIM + 1]

    items = [Item(t, c) for t in range(seq // tq) for c in range(2)]
    n_items = len(items)
    outs = {}

    def finish(i):
        item = items[i]
        outs[i] = item.value_out()
        if item.c == 1:
            o = outs.pop(i - 1) - lam * outs.pop(i)
            ms = jnp.mean(o * o, axis=0, keepdims=True)
            o = o * lax.rsqrt(ms + RMS_EPS) * nw_ref[...] * (1.0 - LAMBDA_INIT)
            o_ref[0, tq * item.t:tq * (item.t + 1), :] = o.T.astype(o_ref.dtype)

    for stage in range(-2, n_items):
        lanes = []
        if 0 <= stage + 2 < n_items:
            lanes.append(items[stage + 2].score_steps())
        if 0 <= stage + 1 < n_items:
            lanes.append(items[stage + 1].prob_steps())
        if 0 <= stage:
            lanes.append(items[stage].value_steps() + [functools.partial(finish, stage)])
        for k in range(max(len(steps) for steps in lanes)):
            for steps in lanes:
                if k < len(steps):
                    steps[k]()


def _attention(slopes, proj3, proj_meta, lam4, nw_col, w_up, w_down):
    nb, seq, _ = proj3.shape
    steps = nb * ATT_HEADS
    wu3 = w_up.reshape(steps, D_MODEL // steps, D_FF)
    wd3 = w_down.reshape(steps, D_FF // steps, D_MODEL)
    w_blk = lambda a: pl.BlockSpec((1,) + a.shape[1:], lambda b, h: (b * ATT_HEADS + h, 0, 0))
    cq, ck, cv = COL_Q // ATT_V_DIM, COL_K // ATT_V_DIM, COL_V // ATT_V_DIM
    kv_rows = CHUNK + seq
    seq_blk = lambda c0: pl.BlockSpec((1, seq, ATT_V_DIM), lambda b, h: (b, 0, c0 + h))
    meta_blk = lambda c0: pl.BlockSpec((CHUNK, ATT_V_DIM), lambda b, h: (0, c0 + h))
    return pl.pallas_call(
        _attn_kernel,
        grid=(nb, ATT_HEADS),
        in_specs=[
            pl.BlockSpec(memory_space=pltpu.SMEM),
            seq_blk(cq), seq_blk(ck), seq_blk(cv), meta_blk(ck), meta_blk(cv),
            pl.BlockSpec((4, ATT_QK_DIM), lambda b, h: (0, 0)),
            pl.BlockSpec((ATT_V_DIM, 1), lambda b, h: (0, 0)),
            w_blk(wu3), w_blk(wd3),
        ],
        out_specs=[pl.BlockSpec((1, seq, ATT_V_DIM), lambda b, h: (b, 0, h)), w_blk(wu3), w_blk(wd3)],
        out_shape=[jax.ShapeDtypeStruct((nb, seq, D_ATT), BF16),
                   jax.ShapeDtypeStruct(wu3.shape, BF16), jax.ShapeDtypeStruct(wd3.shape, BF16)],
        scratch_shapes=[
            pltpu.VMEM((kv_rows, 2 * ATT_V_DIM), BF16),
            pltpu.VMEM((ATT_V_DIM + 16, kv_rows), BF16),
            pltpu.VMEM((2, kv_rows, ATT_TQ), F32),
            pltpu.VMEM((2, kv_rows, ATT_TQ), BF16),
        ],
        compiler_params=pltpu.CompilerParams(
            dimension_semantics=("arbitrary", "arbitrary"), vmem_limit_bytes=VMEM_LIMIT),
        name="diffattn",
    )(slopes, proj3, proj3, proj3, proj_meta, proj_meta, lam4, nw_col, wu3, wd3)


def _outproj_kernel(y_ref, o_ref, wy_ref, wo_ref, mix_ref, wyb_ref, wob_ref):
    @pl.when(pl.program_id(1) == 0)
    def _():
        wyb_ref[...] = wy_ref[...].astype(BF16)
        wob_ref[...] = wo_ref[...].astype(BF16)

    mix_ref[...] = (jnp.dot(y_ref[...], wyb_ref[...], preferred_element_type=F32)
                    + jnp.dot(o_ref[...], wob_ref[...], preferred_element_type=F32))


def _outproj(y2d, o2d, w_out, *, tm, tn):
    m = y2d.shape[0]
    return pl.pallas_call(
        _outproj_kernel,
        grid=(D_MODEL // tn, m // tm),
        in_specs=[
            pl.BlockSpec((tm, D_SSM), lambda j, i: (i, 0)),
            pl.BlockSpec((tm, D_ATT), lambda j, i: (i, 0)),
            pl.BlockSpec((D_SSM, tn), lambda j, i: (0, j)),
            pl.BlockSpec((D_ATT, tn), lambda j, i: (1, j)),
        ],
        out_specs=pl.BlockSpec((tm, tn), lambda j, i: (i, j)),
        out_shape=jax.ShapeDtypeStruct((m, D_MODEL), F32),
        scratch_shapes=[pltpu.VMEM((D_SSM, tn), BF16), pltpu.VMEM((D_ATT, tn), BF16)],
        compiler_params=pltpu.CompilerParams(
            dimension_semantics=("arbitrary", "arbitrary"), vmem_limit_bytes=VMEM_LIMIT),
        name="outproj",
    )(y2d, o2d, w_out, w_out)


def _mlp_kernel(x_ref, mix_ref, wu_ref, wd_ref, g0_ref, b0_ref, g1_ref, b1_ref, g2_ref, b2_ref,
                o_ref, h1_ref, hb_ref, *, ln_rows):
    f = pl.program_id(1)
    n_ln = x_ref.shape[0] // ln_rows

    @pl.when(f == 0)
    def _():
        def body(r, carry):
            rows = pl.ds(pl.multiple_of(r * ln_rows, ln_rows), ln_rows)
            h0 = _layer_norm_rows(x_ref[rows, :], g0_ref[...], b0_ref[...])
            h1 = _layer_norm_rows(ALPHA * h0 + mix_ref[rows, :], g1_ref[...], b1_ref[...])
            h1_ref[rows, :] = h1
            hb_ref[rows, :] = h1.astype(BF16)
            return carry

        lax.fori_loop(0, n_ln, body, 0)
        o_ref[...] = jnp.zeros(o_ref.shape, F32)

    u = jnp.dot(hb_ref[...], wu_ref[...], preferred_element_type=F32)
    u = jnp.square(jnp.maximum(u, 0.0)).astype(BF16)
    for c0 in range(0, D_MODEL, MLP_ACC_SLAB):
        cols = slice(c0, c0 + MLP_ACC_SLAB)
        o_ref[:, cols] += jnp.dot(u, wd_ref[:, cols], preferred_element_type=F32)

    @pl.when(f == pl.num_programs(1) - 1)
    def _():
        def body(r, carry):
            rows = pl.ds(pl.multiple_of(r * ln_rows, ln_rows), ln_rows)
            o_ref[rows, :] = _layer_norm_rows(ALPHA * h1_ref[rows, :] + o_ref[rows, :], g2_ref[...], b2_ref[...])
            return carry

        lax.fori_loop(0, n_ln, body, 0)


def _mlp(x2d, mix, w_up, w_down, g0, b0, g1, b1, g2, b2, *, tm, tf):
    m = x2d.shape[0]
    row = pl.BlockSpec((1, D_MODEL), lambda i, f: (0, 0))
    return pl.pallas_call(
        functools.partial(_mlp_kernel, ln_rows=128),
        grid=(m // tm, D_FF // tf),
        in_specs=[
            pl.BlockSpec((tm, D_MODEL), lambda i, f: (i, 0)),
            pl.BlockSpec((tm, D_MODEL), lambda i, f: (i, 0)),
            pl.BlockSpec((D_MODEL, tf), lambda i, f: (0, f)),
            pl.BlockSpec((tf, D_MODEL), lambda i, f: (f, 0)),
            row, row, row, row, row, row,
        ],
        out_specs=pl.BlockSpec((tm, D_MODEL), lambda i, f: (i, 0)),
        out_shape=jax.ShapeDtypeStruct((m, D_MODEL), F32),
        scratch_shapes=[pltpu.VMEM((tm, D_MODEL), F32), pltpu.VMEM((tm, D_MODEL), BF16)],
        compiler_params=pltpu.CompilerParams(
            dimension_semantics=("arbitrary", "arbitrary"), vmem_limit_bytes=VMEM_LIMIT),
        name="mlp",
    )(x2d, mix, w_up, w_down, g0, b0, g1, b1, g2, b2)


def kernel(x, meta_tokens, ln0_g, ln0_b, w_in, conv_w, conv_b, dt_bias, a_log, d_skip, ssd_norm_w,
           lambda_q1, lambda_k1, lambda_q2, lambda_k2, attn_norm_w, w_out, ln1_g, ln1_b, w_up, w_down,
           ln2_g, ln2_b):
    nb, seq, d = x.shape
    assert (d, w_in.shape[0]) == (D_MODEL, DEPTH) and seq % ATT_TQ == 0
    m = nb * seq
    x2d = x.reshape(m, d)
    row = lambda v: v.reshape(1, -1).astype(F32)

    w_in_t = w_in[0].astype(F32).T
    wdt = w_in_t[W_IN_DT:W_IN_DT + SSM_HEADS].reshape(SSM_GROUPS, SSM_HPG, d)
    wdt = jnp.pad(wdt, ((0, 0), (0, DT_ROWS - SSM_HPG), (0, 0))).reshape(SSM_GROUPS * DT_ROWS, d).astype(BF16)
    pad_heads = lambda v: jnp.pad(v.reshape(SSM_GROUPS, SSM_HPG).astype(F32),
                                  ((0, 0), (0, DT_ROWS - SSM_HPG)))[..., None]
    dtb3, alog3 = pad_heads(dt_bias[0]), pad_heads(a_log[0])
    dskip_row = row(jnp.repeat(d_skip[0], SSM_HEAD_DIM))
    lam4 = jnp.stack([lambda_q1[0], lambda_k1[0], lambda_q2[0], lambda_k2[0]]).astype(F32)
    slopes = jnp.asarray(2.0 ** (-8.0 * np.arange(1, ATT_HEADS + 1) / ATT_HEADS), dtype=F32)
    g0, b0 = row(ln0_g), row(ln0_b)
    meta_pad = jnp.pad(meta_tokens.astype(F32), ((PAD, 0), (0, 0)))

    hn, dt_t, hm, dt_t_meta = _ln0(x2d, meta_pad, g0, b0, wdt, tm=1024)
    wt3 = w_in_t.reshape(-1, SSM_HEADS, d)
    qkv, qkv_meta = _inproj(hn, hm, wt3, tm=2048, tn=768)
    qkv3 = qkv.reshape(nb, seq, N_QKV)
    dt3 = dt_t.reshape(SSM_GROUPS, DT_ROWS, m)
    dt3_meta = dt_t_meta.reshape(SSM_GROUPS, DT_ROWS, CHUNK)

    y = _ssd(hn, hm, wt3, dt3, dt3_meta, conv_w[0].astype(F32), row(conv_b[0]),
             dtb3, alog3, dskip_row, row(ssd_norm_w[0]), nb=nb)
    o, w_up_b, w_down_b = _attention(slopes, qkv3, qkv_meta, lam4,
                                     attn_norm_w[0].reshape(ATT_V_DIM, 1).astype(F32),
                                     w_up[0].astype(F32), w_down[0].astype(F32))

    mix = _outproj(y.reshape(m, D_SSM), o.reshape(m, D_ATT), w_out[0].astype(F32), tm=1024, tn=512)
    h2 = _mlp(x2d, mix, w_up_b.reshape(D_MODEL, D_FF), w_down_b.reshape(D_FF, D_MODEL),
              g0, b0, row(ln1_g[0]), row(ln1_b[0]), row(ln2_g[0]), row(ln2_b[0]), tm=512, tf=1024)
    return h2.reshape(nb, seq, d)
```

```python
import functools
import math

import jax
import jax.numpy as jnp
import numpy as np
from jax import lax
from jax.experimental import pallas as pl
from jax.experimental.pallas import tpu as pltpu

F32 = jnp.float32
BF16 = jnp.bfloat16

D_MODEL = 2048
N_META = 16
CHUNK = 128
PAD = CHUNK - N_META
D_SSM = 2048
D_ATT = 2048
SSM_HEAD_DIM = 64
SSM_HEADS = 32
SSM_GROUPS = 8
SSM_HPG = 4
GROUP_W = SSM_HPG * SSM_HEAD_DIM
D_STATE = 128
CONV_K = 4
D_CONV = D_SSM + 2 * SSM_GROUPS * D_STATE
ATT_V_DIM = 128
ATT_HEADS = 16
ATT_QK_DIM = 64
D_FF = 4 * D_MODEL
DEPTH = 1
ALPHA = (2 * DEPTH) ** 0.25
LN_EPS = 1e-5
RMS_EPS = 1e-5
NEG_INF = -1e30
LOG2E = math.log2(math.e)
LAMBDA_INIT = 0.8 - 0.6 * math.exp(-0.3 * 0)

COL_Z = 0
COL_X = D_SSM
COL_B = COL_X + D_SSM
COL_C = COL_B + SSM_GROUPS * D_STATE
W_IN_DT = D_SSM + D_CONV
W_IN_Q = W_IN_DT + SSM_HEADS
N_QKV = 3 * D_ATT
COL_Q = 0
COL_K = D_ATT
COL_V = 2 * D_ATT
DT_ROWS = 8

V7X_VMEM_BYTES = 64 * 1024 * 1024
VMEM_LIMIT = 56 * 1024 * 1024

ATT_HEADS_PER_STEP = 2
ATT_TQ = 512
MLP_ACC_SLAB = 512


def _layer_norm_rows(x, g, b):
    mu = jnp.mean(x, axis=-1, keepdims=True)
    xc = x - mu
    var = jnp.mean(xc * xc, axis=-1, keepdims=True)
    return xc * lax.rsqrt(var + LN_EPS) * g + b


def _silu(x):
    hx = 0.5 * x
    return hx + hx * jnp.tanh(hx)


def _softplus(x):
    return jnp.maximum(x, 0.0) + jnp.log1p(jnp.exp(-jnp.abs(x)))


CONTRACT_LAST = (((1,), (1,)), ((), ()))


def _ln0_kernel(x_ref, xm_ref, g_ref, b_ref, wdt_ref, hn_ref, dt_ref, hm_ref, dtm_ref, *, ln_rows):
    def body(r, carry):
        r0 = pl.multiple_of(r * ln_rows, ln_rows)
        hn = _layer_norm_rows(x_ref[pl.ds(r0, ln_rows), :], g_ref[...], b_ref[...])
        hn_ref[pl.ds(r0, ln_rows), :] = hn.astype(BF16)
        return carry

    lax.fori_loop(0, x_ref.shape[0] // ln_rows, body, 0)
    dt_ref[...] = lax.dot_general(wdt_ref[...], hn_ref[...], CONTRACT_LAST, preferred_element_type=F32)

    @pl.when(pl.program_id(0) == 0)
    def _():
        hm_ref[...] = _layer_norm_rows(xm_ref[...], g_ref[...], b_ref[...]).astype(BF16)
        dtm_ref[...] = lax.dot_general(wdt_ref[...], hm_ref[...], CONTRACT_LAST, preferred_element_type=F32)


def _ln0(x2d, meta_pad, g, b, wdt, *, tm):
    m = x2d.shape[0]
    ndt = wdt.shape[0]
    const = lambda i: (0, 0)
    return pl.pallas_call(
        functools.partial(_ln0_kernel, ln_rows=128),
        grid=(m // tm,),
        in_specs=[
            pl.BlockSpec((tm, D_MODEL), lambda i: (i, 0)),
            pl.BlockSpec((CHUNK, D_MODEL), const),
            pl.BlockSpec((1, D_MODEL), const),
            pl.BlockSpec((1, D_MODEL), const),
            pl.BlockSpec((ndt, D_MODEL), const),
        ],
        out_specs=[
            pl.BlockSpec((tm, D_MODEL), lambda i: (i, 0)),
            pl.BlockSpec((ndt, tm), lambda i: (0, i)),
            pl.BlockSpec((CHUNK, D_MODEL), const),
            pl.BlockSpec((ndt, CHUNK), const),
        ],
        out_shape=[
            jax.ShapeDtypeStruct((m, D_MODEL), BF16),
            jax.ShapeDtypeStruct((ndt, m), F32),
            jax.ShapeDtypeStruct((CHUNK, D_MODEL), BF16),
            jax.ShapeDtypeStruct((ndt, CHUNK), F32),
        ],
        compiler_params=pltpu.CompilerParams(dimension_semantics=("arbitrary",), vmem_limit_bytes=VMEM_LIMIT),
        name="ln0",
    )(x2d, meta_pad, g, b, wdt)


def _inproj_kernel(hn_ref, hm_ref, wt_ref, o_ref, om_ref, wb_ref):
    j = pl.program_id(0)
    tn = wb_ref.shape[0]
    new_tile = pl.program_id(1) == 0
    unused = j < COL_K // tn

    @pl.when(new_tile)
    def _():
        wb_ref[...] = wt_ref[...].reshape(wb_ref.shape).astype(BF16)

    @pl.when(new_tile & jnp.logical_not(unused))
    def _():
        om_ref[...] = lax.dot_general(hm_ref[...], wb_ref[...], CONTRACT_LAST,
                                      preferred_element_type=F32).astype(om_ref.dtype)

    @pl.when(new_tile & unused)
    def _():
        om_ref[...] = jnp.zeros(om_ref.shape, om_ref.dtype)

    o_ref[...] = lax.dot_general(hn_ref[...], wb_ref[...], CONTRACT_LAST,
                                 preferred_element_type=F32).astype(o_ref.dtype)


def _inproj(hn, hm, wt, *, tm, tn):
    m = hn.shape[0]
    n = N_QKV
    assert n % tn == 0 and tn % SSM_HEADS == 0 and wt.shape[1] == SSM_HEADS and W_IN_Q % SSM_HEADS == 0
    groups = tn // SSM_HEADS
    group_start = lambda j: W_IN_Q // SSM_HEADS + j * groups
    return pl.pallas_call(
        _inproj_kernel,
        grid=(n // tn, m // tm),
        in_specs=[
            pl.BlockSpec((tm, D_MODEL), lambda j, i: (i, 0)),
            pl.BlockSpec((CHUNK, D_MODEL), lambda j, i: (0, 0)),
            pl.BlockSpec((pl.Element(groups), pl.Element(SSM_HEADS), pl.Element(D_MODEL)),
                         lambda j, i: (group_start(j), 0, 0)),
        ],
        out_specs=[
            pl.BlockSpec((tm, tn), lambda j, i: (i, j)),
            pl.BlockSpec((CHUNK, tn), lambda j, i: (0, j)),
        ],
        out_shape=[
            jax.ShapeDtypeStruct((m, n), BF16),
            jax.ShapeDtypeStruct((CHUNK, n), BF16),
        ],
        scratch_shapes=[pltpu.VMEM((tn, D_MODEL), BF16)],
        compiler_params=pltpu.CompilerParams(
            dimension_semantics=("arbitrary", "arbitrary"), vmem_limit_bytes=VMEM_LIMIT),
        name="inproj",
    )(hn, hm, wt)


U_LEAD = 16
U_SEQ0 = U_LEAD + CHUNK


def _conv_silu(u_ref, r0, w_ref, b_ref):
    win = u_ref[pl.ds(r0 - 8, CHUNK + 8), :]
    w = w_ref[...]
    acc = b_ref[...] + w[3:4, :] * win[8:CHUNK + 8, :]
    acc = acc + w[2:3, :] * win[7:CHUNK + 7, :]
    acc = acc + w[1:2, :] * win[6:CHUNK + 6, :]
    acc = acc + w[0:1, :] * win[5:CHUNK + 5, :]
    return _silu(acc)


SSD_PROJ_ROWS = 512


def _ssd_kernel(hn_ref, hmeta_ref, wz_ref, wx_ref, wb_ref, wc_ref, dt_ref, dtm_ref,
                cwx_ref, cwb_ref, cwc_ref, cbx_ref, cbb_ref, cbc_ref,
                dtb_ref, alog_ref, dskip_ref, nw_ref,
                y_ref, w_ref, z_ref, ux_ref, ub_ref, uc_ref, st_ref, m1_ref, tril_ref, m2_ref, hm_ref):
    seq = hn_ref.shape[0]

    li = lax.broadcasted_iota(jnp.int32, (CHUNK, CHUNK), 0)
    ki = lax.broadcasted_iota(jnp.int32, (CHUNK, CHUNK), 1)
    lower = jnp.where(ki <= li, 1.0, 0.0)
    tril_ref[...] = lower
    m1_ref[...] = lower.astype(BF16)
    k2 = lax.broadcasted_iota(jnp.int32, (2 * CHUNK, 2 * CHUNK), 0) & (CHUNK - 1)
    s2 = lax.broadcasted_iota(jnp.int32, (2 * CHUNK, 2 * CHUNK), 1)
    m2_ref[...] = jnp.where((s2 >= CHUNK) | (k2 > s2), 1.0, 0.0).astype(BF16)
    head_of = lax.broadcasted_iota(jnp.int32, (CHUNK, GROUP_W), 1) >> 6
    for r in range(SSM_HPG):
        hm_ref[r] = jnp.where(head_of == r, 1.0, 0.0).astype(BF16)

    nz, nx, nbc = GROUP_W, GROUP_W, D_STATE
    w_ref[0:nz, :] = wz_ref[...].reshape(nz, D_MODEL).astype(BF16)
    w_ref[nz:nz + nx, :] = wx_ref[...].reshape(nx, D_MODEL).astype(BF16)
    w_ref[nz + nx:nz + nx + nbc, :] = wb_ref[...].reshape(nbc, D_MODEL).astype(BF16)
    w_ref[nz + nx + nbc:, :] = wc_ref[...].reshape(nbc, D_MODEL).astype(BF16)

    for u_ref in (ux_ref, ub_ref, uc_ref):
        u_ref[0:U_LEAD, :] = jnp.zeros((U_LEAD, u_ref.shape[1]), F32)
    st_ref[...] = jnp.zeros(st_ref.shape, F32)
    meta_valid = lax.broadcasted_iota(jnp.int32, (CHUNK, 1), 0) >= PAD

    def project(lhs, u_row0, z_row0, meta_rows):
        res = lax.dot_general(lhs, w_ref[...], CONTRACT_LAST, preferred_element_type=F32)
        if meta_rows:
            head = jnp.where(meta_valid, res[:meta_rows], 0.0)
            res_u = jnp.concatenate([head, res[meta_rows:]], axis=0)
        else:
            res_u = res
        n = res.shape[0]
        ux_ref[u_row0:u_row0 + n, :] = res_u[:, nz:nz + nx]
        ub_ref[u_row0:u_row0 + n, :] = res_u[:, nz + nx:nz + nx + nbc]
        uc_ref[u_row0:u_row0 + n, :] = res_u[:, nz + nx + nbc:]
        z_ref[z_row0:z_row0 + n - meta_rows, :] = res[meta_rows:, 0:nz]

    def project_block(blk):
        r0 = blk * SSD_PROJ_ROWS
        if blk == 0:
            lhs = jnp.concatenate([hmeta_ref[...], hn_ref[0:SSD_PROJ_ROWS, :]], axis=0)
            project(lhs, U_LEAD, 0, CHUNK)
        else:
            project(hn_ref[r0:r0 + SSD_PROJ_ROWS, :], U_SEQ0 + r0, r0, 0)

    dt_bias = dtb_ref[0]
    a_neg = -jnp.exp(alog_ref[0])

    def chunk_step(r0, dt8, z_rows, out_row0):
        xc = _conv_silu(ux_ref, r0, cwx_ref, cbx_ref)
        bc = _conv_silu(ub_ref, r0, cwb_ref, cbb_ref)
        a8 = dt8 * a_neg
        xs_bf = xc.astype(BF16)
        bt = bc.T
        st_old = st_ref[...]
        want_y = z_rows is not None
        if want_y:
            cc = _conv_silu(uc_ref, r0, cwc_ref, cbc_ref)
            cb = lax.dot_general(cc.astype(BF16), bc.astype(BF16), CONTRACT_LAST,
                                 preferred_element_type=F32)
            cb_low = cb * tril_ref[...]
            st_bf = st_old.astype(BF16)
        lhs_y, rhs_y, lhs_s, rhs_s, tot = [], [], [], [], []
        for r in range(SSM_HPG):
            a_r = a8[r:r + 1, :]
            dt_r = dt8[r:r + 1, :]
            a_hi = a_r.astype(BF16)
            a_lo = (a_r - a_hi.astype(F32)).astype(BF16)
            m1 = m1_ref[...]
            dd = jnp.dot(jnp.concatenate([m1 * a_hi, m1 * a_lo], axis=1), m2_ref[...],
                         preferred_element_type=F32)
            dseg = dd[:, :CHUNK]
            acsb = dd[:, CHUNK:]
            lmat = jnp.exp(dseg)
            xs_r = xs_bf * hm_ref[r]
            if want_y:
                lhs_y.append((cb_low * lmat * dt_r).astype(BF16))
                lhs_y.append((cc * jnp.exp(acsb)).astype(BF16))
                rhs_y.append(xs_r)
                rhs_y.append(st_bf * hm_ref[r])
            f1 = lmat[CHUNK - 1:CHUNK, :] * dt_r
            lhs_s.append((bt * f1).astype(BF16))
            rhs_s.append(xs_r)
            tot.append(acsb[CHUNK - 1:CHUNK, :])
        s_new = jnp.dot(jnp.concatenate(lhs_s, axis=1), jnp.concatenate(rhs_s, axis=0),
                        preferred_element_type=F32)
        half = lax.broadcasted_iota(jnp.int32, (1, CHUNK), 1) < SSM_HEAD_DIM
        decay = jnp.exp(jnp.concatenate([jnp.where(half, tot[0], tot[1]),
                                         jnp.where(half, tot[2], tot[3])], axis=1))
        st_ref[...] = st_old * decay + s_new
        if want_y:
            y = jnp.dot(jnp.concatenate(lhs_y, axis=1), jnp.concatenate(rhs_y, axis=0),
                        preferred_element_type=F32)
            y = y + xc * dskip_ref[...]
            gated = y * _silu(z_rows.astype(F32))
            ms = jnp.mean(gated * gated, axis=-1, keepdims=True)
            y_ref[0, out_row0:out_row0 + CHUNK, :] = (gated * lax.rsqrt(ms + RMS_EPS) * nw_ref[...]).astype(y_ref.dtype)

    def meta_chunk():
        lane_pos = lax.broadcasted_iota(jnp.int32, (1, CHUNK), 1)
        dt_meta = jnp.where(lane_pos >= PAD, _softplus(dtm_ref[0] + dt_bias), 0.0)
        chunk_step(U_LEAD, dt_meta, None, None)

    def seq_chunk(s):
        row0 = s * CHUNK
        dt8 = _softplus(dt_ref[0, :, row0:row0 + CHUNK] + dt_bias)
        chunk_step(U_SEQ0 + row0, dt8, z_ref[row0:row0 + CHUNK, :], row0)

    per_blk = SSD_PROJ_ROWS // CHUNK
    n_blk = seq // SSD_PROJ_ROWS
    project_block(0)
    project_block(1)
    for blk in range(n_blk):
        if blk + 2 < n_blk:
            project_block(blk + 2)
        if blk == 0:
            meta_chunk()
        for s in range(blk * per_blk, (blk + 1) * per_blk):
            seq_chunk(s)


def _ssd(hn, hm, wt3, dt3, dt3_meta, conv_w, conv_b2, dtb3, alog3, dskip_row, nw_row, *, nb):
    m = hn.shape[0]
    seq = m // nb
    u_rows = U_SEQ0 + seq
    grp = lambda col0, width: (lambda b, g: ((col0 + g * width) // SSM_HEADS, 0, 0))
    w_blk = lambda col0, width: pl.BlockSpec(
        (pl.Element(width // SSM_HEADS), pl.Element(SSM_HEADS), pl.Element(D_MODEL)), grp(col0, width))
    return pl.pallas_call(
        _ssd_kernel,
        grid=(nb, SSM_GROUPS),
        in_specs=[
            pl.BlockSpec((seq, D_MODEL), lambda b, g: (b, 0)),
            pl.BlockSpec((CHUNK, D_MODEL), lambda b, g: (0, 0)),
            w_blk(COL_Z, GROUP_W), w_blk(COL_X, GROUP_W), w_blk(COL_B, D_STATE), w_blk(COL_C, D_STATE),
            pl.BlockSpec((1, DT_ROWS, seq), lambda b, g: (g, 0, b)),
            pl.BlockSpec((1, DT_ROWS, CHUNK), lambda b, g: (g, 0, 0)),
            pl.BlockSpec((CONV_K, GROUP_W), lambda b, g: (0, g)),
            pl.BlockSpec((CONV_K, D_STATE), lambda b, g: (0, D_SSM // D_STATE + g)),
            pl.BlockSpec((CONV_K, D_STATE), lambda b, g: (0, D_SSM // D_STATE + SSM_GROUPS + g)),
            pl.BlockSpec((1, GROUP_W), lambda b, g: (0, g)),
            pl.BlockSpec((1, D_STATE), lambda b, g: (0, D_SSM // D_STATE + g)),
            pl.BlockSpec((1, D_STATE), lambda b, g: (0, D_SSM // D_STATE + SSM_GROUPS + g)),
            pl.BlockSpec((1, DT_ROWS, 1), lambda b, g: (g, 0, 0)),
            pl.BlockSpec((1, DT_ROWS, 1), lambda b, g: (g, 0, 0)),
            pl.BlockSpec((1, GROUP_W), lambda b, g: (0, g)),
            pl.BlockSpec((1, GROUP_W), lambda b, g: (0, g)),
        ],
        out_specs=pl.BlockSpec((1, seq, GROUP_W), lambda b, g: (b, 0, g)),
        out_shape=jax.ShapeDtypeStruct((nb, seq, D_SSM), BF16),
        scratch_shapes=[
            pltpu.VMEM((2 * GROUP_W + 2 * D_STATE, D_MODEL), BF16),
            pltpu.VMEM((seq, GROUP_W), F32),
            pltpu.VMEM((u_rows, GROUP_W), F32),
            pltpu.VMEM((u_rows, D_STATE), F32),
            pltpu.VMEM((u_rows, D_STATE), F32),
            pltpu.VMEM((D_STATE, GROUP_W), F32),
            pltpu.VMEM((CHUNK, CHUNK), BF16),
            pltpu.VMEM((CHUNK, CHUNK), F32),
            pltpu.VMEM((2 * CHUNK, 2 * CHUNK), BF16),
            pltpu.VMEM((SSM_HPG, CHUNK, GROUP_W), BF16),
        ],
        compiler_params=pltpu.CompilerParams(
            dimension_semantics=("arbitrary", "arbitrary"), vmem_limit_bytes=VMEM_LIMIT),
        name="ssd",
    )(hn, hm, wt3, wt3, wt3, wt3, dt3, dt3_meta,
      conv_w, conv_w, conv_w, conv_b2, conv_b2, conv_b2, dtb3, alog3, dskip_row, nw_row)


def _attn_kernel(slope_ref, q_ref, k_ref, v_ref, km_ref, vm_ref, lam_ref, nw_ref, wu_ref, wd_ref,
                 o_ref, wub_ref, wdb_ref, kaug_ref, vt_ref, s_ref, p_ref):
    wub_ref[...] = wu_ref[...].astype(BF16)
    wdb_ref[...] = wd_ref[...].astype(BF16)

    hp = pl.program_id(1)
    seq = q_ref.shape[1]
    tq = ATT_TQ
    slopes = [slope_ref[ATT_HEADS_PER_STEP * hp + hh] for hh in range(ATT_HEADS_PER_STEP)]
    head_lanes = [slice(ATT_V_DIM * hh, ATT_V_DIM * (hh + 1)) for hh in range(ATT_HEADS_PER_STEP)]

    sub8 = lax.broadcasted_iota(jnp.int32, (8, CHUNK), 0)
    lane8 = lax.broadcasted_iota(jnp.int32, (8, CHUNK), 1)

    def stage(hh, r0, k, v, valid_from):
        pos = lane8 + r0
        bias = (slopes[hh] * LOG2E) * pos.astype(F32)
        if valid_from:
            bias = jnp.where(pos >= valid_from, bias, NEG_INF)
        b1 = bias.astype(BF16).astype(F32)
        b2 = (bias - b1).astype(BF16).astype(F32)
        b3 = bias - b1 - b2
        rows8 = jnp.where(sub8 == 0, b1, jnp.where(sub8 == 1, b2, jnp.where(sub8 == 2, b3, 0.0)))
        cols = jnp.concatenate([rows8, jnp.zeros((CHUNK - 8, CHUNK), F32)], axis=0).T
        kaug_ref[hh, r0:r0 + CHUNK, 0:128] = (k.astype(F32) * LOG2E).astype(BF16)
        kaug_ref[hh, r0:r0 + CHUNK, 128:256] = cols.astype(BF16)
        vt_ref[hh, 0:ATT_V_DIM, r0:r0 + CHUNK] = v.T

    for hh, hl in enumerate(head_lanes):
        vt_ref[hh, ATT_V_DIM:, :] = jnp.ones((vt_ref.shape[1] - ATT_V_DIM, vt_ref.shape[2]), BF16)
        stage(hh, 0, km_ref[:, hl], vm_ref[:, hl], PAD)
        for j in range(seq // CHUNK):
            rows = slice(CHUNK * j, CHUNK * (j + 1))
            stage(hh, CHUNK * (j + 1), k_ref[0, rows, hl], v_ref[0, rows, hl], 0)

    lam4 = lam_ref[...]
    lam = (jnp.exp(jnp.sum(lam4[0:1] * lam4[1:2], axis=-1, keepdims=True))
           - jnp.exp(jnp.sum(lam4[2:3] * lam4[3:4], axis=-1, keepdims=True)) + LAMBDA_INIT)

    lane_q = lax.broadcasted_iota(jnp.int32, (tq, 128), 1)
    ones_blk = jnp.where(lane_q < 3, 1.0, 0.0).astype(BF16)
    map_lanes = (lane_q < ATT_QK_DIM, lane_q >= ATT_QK_DIM)
    hq = tq // 2
    tri = (lax.broadcasted_iota(jnp.int32, (hq, hq), 0)
           <= lax.broadcasted_iota(jnp.int32, (hq, hq), 1))

    def scores(hh, rows, qa):
        return lax.dot_general(kaug_ref[hh, rows, :], qa, CONTRACT_LAST, preferred_element_type=F32)

    class Item:
        def __init__(self, t, c, hh):
            self.t, self.c, self.hh = t, c, hh
            self.slot = 2 * hh + c
            d0 = CHUNK + tq * t
            self.kend = d0 + tq
            self.rows_a, self.rows_b = slice(d0, d0 + hq), slice(d0 + hq, self.kend)
            self.blocks = [slice(0, CHUNK)] + [slice(CHUNK + tq * j, CHUNK + tq * (j + 1)) for j in range(t)]
            self.m = None
            self.qaug = None

        def score_steps(self):
            c, hh, slot = self.c, self.hh, self.slot

            def first():
                q = q_ref[0, tq * self.t:tq * (self.t + 1), head_lanes[hh]] * jnp.asarray(ATT_QK_DIM ** -0.5, BF16)
                self.qaug = jnp.concatenate([jnp.where(map_lanes[c], q, jnp.zeros_like(q)), ones_blk], axis=1)

            def full(rows):
                def step():
                    sv = scores(hh, rows, self.qaug)
                    s_ref[slot,rows, :] = sv
                    bm = jnp.max(sv, axis=0, keepdims=True)
                    self.m = bm if self.m is None else jnp.maximum(self.m, bm)
                return step

            def diag_a():
                sv = scores(hh, self.rows_a, self.qaug)
                sv = jnp.concatenate([jnp.where(tri, sv[:, :hq], NEG_INF), sv[:, hq:]], axis=1)
                s_ref[slot,self.rows_a, :] = sv
                self.m = jnp.maximum(self.m, jnp.max(sv, axis=0, keepdims=True))

            def diag_b():
                sv = jnp.where(tri, scores(hh, self.rows_b, self.qaug[hq:, :]), NEG_INF)
                s_ref[slot,self.rows_b, hq:] = sv
                m = self.m
                self.m = jnp.concatenate(
                    [m[:, :hq], jnp.maximum(m[:, hq:], jnp.max(sv, axis=0, keepdims=True))], axis=1)

            return [first] + [full(r) for r in self.blocks] + [diag_a, diag_b]

        def prob_steps(self):
            slot = self.slot

            def full(rows):
                def step():
                    p_ref[slot,rows, :] = jnp.exp2(s_ref[slot,rows, :] - self.m).astype(BF16)
                return step

            def diag_b():
                p_ref[slot,self.rows_b, :hq] = jnp.zeros((hq, hq), BF16)
                p_ref[slot,self.rows_b, hq:] = jnp.exp2(s_ref[slot,self.rows_b, hq:] - self.m[:, hq:]).astype(BF16)

            return [full(r) for r in self.blocks + [self.rows_a]] + [diag_b]

        def value_steps(self):
            self.acc = None

            def part(rows):
                def step():
                    d = jnp.dot(vt_ref[self.hh, :, rows], p_ref[self.slot, rows, :], preferred_element_type=F32)
                    self.acc = d if self.acc is None else self.acc + d
                return step

            return [part(r) for r in self.blocks + [slice(self.rows_a.start, self.kend)]]

        def value_out(self):
            return self.acc[:ATT_V_DIM] / self.acc[ATT_V_DIM:ATT_V_DIM + 1]

    items = [Item(t, c, hh) for t in range(seq // tq) for c in range(2) for hh in range(ATT_HEADS_PER_STEP)]
    n_items = len(items)
    outs = {}

    def finish(i):
        item = items[i]
        outs[(item.t, item.hh, item.c)] = item.value_out()
        if item.c == 1:
            o = outs.pop((item.t, item.hh, 0)) - lam * outs.pop((item.t, item.hh, 1))
            ms = jnp.mean(o * o, axis=0, keepdims=True)
            o = o * lax.rsqrt(ms + RMS_EPS) * nw_ref[...] * (1.0 - LAMBDA_INIT)
            o_ref[0, tq * item.t:tq * (item.t + 1), head_lanes[item.hh]] = o.T.astype(o_ref.dtype)

    for stage in range(-2, n_items):
        lanes = []
        if 0 <= stage + 2 < n_items:
            lanes.append(items[stage + 2].score_steps())
        if 0 <= stage + 1 < n_items:
            lanes.append(items[stage + 1].prob_steps())
        if 0 <= stage:
            lanes.append(items[stage].value_steps() + [functools.partial(finish, stage)])
        for k in range(max(len(steps) for steps in lanes)):
            for steps in lanes:
                if k < len(steps):
                    steps[k]()


def _attention(slopes, proj3, proj_meta, lam4, nw_col, w_up, w_down):
    nb, seq, _ = proj3.shape
    steps = nb * ATT_HEADS // ATT_HEADS_PER_STEP
    pair_w = ATT_V_DIM * ATT_HEADS_PER_STEP
    wu3 = w_up.reshape(steps, D_MODEL // steps, D_FF)
    wd3 = w_down.reshape(steps, D_FF // steps, D_MODEL)
    w_blk = lambda a: pl.BlockSpec((1,) + a.shape[1:], lambda b, h: (b * (ATT_HEADS // ATT_HEADS_PER_STEP) + h, 0, 0))
    cq, ck, cv = COL_Q // pair_w, COL_K // pair_w, COL_V // pair_w
    kv_rows = CHUNK + seq
    seq_blk = lambda c0: pl.BlockSpec((1, seq, pair_w), lambda b, h: (b, 0, c0 + h))
    meta_blk = lambda c0: pl.BlockSpec((CHUNK, pair_w), lambda b, h: (0, c0 + h))
    return pl.pallas_call(
        _attn_kernel,
        grid=(nb, ATT_HEADS // ATT_HEADS_PER_STEP),
        in_specs=[
            pl.BlockSpec(memory_space=pltpu.SMEM),
            seq_blk(cq), seq_blk(ck), seq_blk(cv), meta_blk(ck), meta_blk(cv),
            pl.BlockSpec((4, ATT_QK_DIM), lambda b, h: (0, 0)),
            pl.BlockSpec((ATT_V_DIM, 1), lambda b, h: (0, 0)),
            w_blk(wu3), w_blk(wd3),
        ],
        out_specs=[pl.BlockSpec((1, seq, pair_w), lambda b, h: (b, 0, h)), w_blk(wu3), w_blk(wd3)],
        out_shape=[jax.ShapeDtypeStruct((nb, seq, D_ATT), BF16),
                   jax.ShapeDtypeStruct(wu3.shape, BF16), jax.ShapeDtypeStruct(wd3.shape, BF16)],
        scratch_shapes=[
            pltpu.VMEM((ATT_HEADS_PER_STEP, kv_rows, 2 * ATT_V_DIM), BF16),
            pltpu.VMEM((ATT_HEADS_PER_STEP, ATT_V_DIM + 16, kv_rows), BF16),
            pltpu.VMEM((2 * ATT_HEADS_PER_STEP, kv_rows, ATT_TQ), F32),
            pltpu.VMEM((2 * ATT_HEADS_PER_STEP, kv_rows, ATT_TQ), BF16),
        ],
        compiler_params=pltpu.CompilerParams(
            dimension_semantics=("arbitrary", "arbitrary"), vmem_limit_bytes=VMEM_LIMIT),
        name="diffattn",
    )(slopes, proj3, proj3, proj3, proj_meta, proj_meta, lam4, nw_col, wu3, wd3)


def _outproj_kernel(y_ref, o_ref, wy_ref, wo_ref, mix_ref, wyb_ref, wob_ref):
    @pl.when(pl.program_id(1) == 0)
    def _():
        wyb_ref[...] = wy_ref[...].astype(BF16)
        wob_ref[...] = wo_ref[...].astype(BF16)

    mix_ref[...] = (jnp.dot(y_ref[...], wyb_ref[...], preferred_element_type=F32)
                    + jnp.dot(o_ref[...], wob_ref[...], preferred_element_type=F32))


def _outproj(y2d, o2d, w_out, *, tm, tn):
    m = y2d.shape[0]
    return pl.pallas_call(
        _outproj_kernel,
        grid=(D_MODEL // tn, m // tm),
        in_specs=[
            pl.BlockSpec((tm, D_SSM), lambda j, i: (i, 0)),
            pl.BlockSpec((tm, D_ATT), lambda j, i: (i, 0)),
            pl.BlockSpec((D_SSM, tn), lambda j, i: (0, j)),
            pl.BlockSpec((D_ATT, tn), lambda j, i: (1, j)),
        ],
        out_specs=pl.BlockSpec((tm, tn), lambda j, i: (i, j)),
        out_shape=jax.ShapeDtypeStruct((m, D_MODEL), F32),
        scratch_shapes=[pltpu.VMEM((D_SSM, tn), BF16), pltpu.VMEM((D_ATT, tn), BF16)],
        compiler_params=pltpu.CompilerParams(
            dimension_semantics=("arbitrary", "arbitrary"), vmem_limit_bytes=VMEM_LIMIT),
        name="outproj",
    )(y2d, o2d, w_out, w_out)


def _mlp_kernel(x_ref, mix_ref, wu_ref, wd_ref, g0_ref, b0_ref, g1_ref, b1_ref, g2_ref, b2_ref,
                o_ref, h1_ref, hb_ref, *, ln_rows):
    f = pl.program_id(1)
    n_ln = x_ref.shape[0] // ln_rows

    @pl.when(f == 0)
    def _():
        def body(r, carry):
            rows = pl.ds(pl.multiple_of(r * ln_rows, ln_rows), ln_rows)
            h0 = _layer_norm_rows(x_ref[rows, :], g0_ref[...], b0_ref[...])
            h1 = _layer_norm_rows(ALPHA * h0 + mix_ref[rows, :], g1_ref[...], b1_ref[...])
            h1_ref[rows, :] = h1
            hb_ref[rows, :] = h1.astype(BF16)
            return carry

        lax.fori_loop(0, n_ln, body, 0)
        o_ref[...] = jnp.zeros(o_ref.shape, F32)

    u = jnp.dot(hb_ref[...], wu_ref[...], preferred_element_type=F32)
    u = jnp.square(jnp.maximum(u, 0.0)).astype(BF16)
    for c0 in range(0, D_MODEL, MLP_ACC_SLAB):
        cols = slice(c0, c0 + MLP_ACC_SLAB)
        o_ref[:, cols] += jnp.dot(u, wd_ref[:, cols], preferred_element_type=F32)

    @pl.when(f == pl.num_programs(1) - 1)
    def _():
        def body(r, carry):
            rows = pl.ds(pl.multiple_of(r * ln_rows, ln_rows), ln_rows)
            o_ref[rows, :] = _layer_norm_rows(ALPHA * h1_ref[rows, :] + o_ref[rows, :], g2_ref[...], b2_ref[...])
            return carry

        lax.fori_loop(0, n_ln, body, 0)


def _mlp(x2d, mix, w_up, w_down, g0, b0, g1, b1, g2, b2, *, tm, tf):
    m = x2d.shape[0]
    row = pl.BlockSpec((1, D_MODEL), lambda i, f: (0, 0))
    return pl.pallas_call(
        functools.partial(_mlp_kernel, ln_rows=128),
        grid=(m // tm, D_FF // tf),
        in_specs=[
            pl.BlockSpec((tm, D_MODEL), lambda i, f: (i, 0)),
            pl.BlockSpec((tm, D_MODEL), lambda i, f: (i, 0)),
            pl.BlockSpec((D_MODEL, tf), lambda i, f: (0, f)),
            pl.BlockSpec((tf, D_MODEL), lambda i, f: (f, 0)),
            row, row, row, row, row, row,
        ],
        out_specs=pl.BlockSpec((tm, D_MODEL), lambda i, f: (i, 0)),
        out_shape=jax.ShapeDtypeStruct((m, D_MODEL), F32),
        scratch_shapes=[pltpu.VMEM((tm, D_MODEL), F32), pltpu.VMEM((tm, D_MODEL), BF16)],
        compiler_params=pltpu.CompilerParams(
            dimension_semantics=("arbitrary", "arbitrary"), vmem_limit_bytes=VMEM_LIMIT),
        name="mlp",
    )(x2d, mix, w_up, w_down, g0, b0, g1, b1, g2, b2)


def kernel(x, meta_tokens, ln0_g, ln0_b, w_in, conv_w, conv_b, dt_bias, a_log, d_skip, ssd_norm_w,
           lambda_q1, lambda_k1, lambda_q2, lambda_k2, attn_norm_w, w_out, ln1_g, ln1_b, w_up, w_down,
           ln2_g, ln2_b):
    nb, seq, d = x.shape
    assert (d, w_in.shape[0]) == (D_MODEL, DEPTH) and seq % ATT_TQ == 0
    m = nb * seq
    x2d = x.reshape(m, d)
    row = lambda v: v.reshape(1, -1).astype(F32)

    w_in_t = w_in[0].astype(F32).T
    wdt = w_in_t[W_IN_DT:W_IN_DT + SSM_HEADS].reshape(SSM_GROUPS, SSM_HPG, d)
    wdt = jnp.pad(wdt, ((0, 0), (0, DT_ROWS - SSM_HPG), (0, 0))).reshape(SSM_GROUPS * DT_ROWS, d).astype(BF16)
    pad_heads = lambda v: jnp.pad(v.reshape(SSM_GROUPS, SSM_HPG).astype(F32),
                                  ((0, 0), (0, DT_ROWS - SSM_HPG)))[..., None]
    dtb3, alog3 = pad_heads(dt_bias[0]), pad_heads(a_log[0])
    dskip_row = row(jnp.repeat(d_skip[0], SSM_HEAD_DIM))
    lam4 = jnp.stack([lambda_q1[0], lambda_k1[0], lambda_q2[0], lambda_k2[0]]).astype(F32)
    slopes = jnp.asarray(2.0 ** (-8.0 * np.arange(1, ATT_HEADS + 1) / ATT_HEADS), dtype=F32)
    g0, b0 = row(ln0_g), row(ln0_b)
    meta_pad = jnp.pad(meta_tokens.astype(F32), ((PAD, 0), (0, 0)))

    hn, dt_t, hm, dt_t_meta = _ln0(x2d, meta_pad, g0, b0, wdt, tm=1024)
    wt3 = w_in_t.reshape(-1, SSM_HEADS, d)
    qkv, qkv_meta = _inproj(hn, hm, wt3, tm=2048, tn=768)
    qkv3 = qkv.reshape(nb, seq, N_QKV)
    dt3 = dt_t.reshape(SSM_GROUPS, DT_ROWS, m)
    dt3_meta = dt_t_meta.reshape(SSM_GROUPS, DT_ROWS, CHUNK)

    y = _ssd(hn, hm, wt3, dt3, dt3_meta, conv_w[0].astype(F32), row(conv_b[0]),
             dtb3, alog3, dskip_row, row(ssd_norm_w[0]), nb=nb)
    o, w_up_b, w_down_b = _attention(slopes, qkv3, qkv_meta, lam4,
                                     attn_norm_w[0].reshape(ATT_V_DIM, 1).astype(F32),
                                     w_up[0].astype(F32), w_down[0].astype(F32))

    mix = _outproj(y.reshape(m, D_SSM), o.reshape(m, D_ATT), w_out[0].astype(F32), tm=1024, tn=512)
    h2 = _mlp(x2d, mix, w_up_b.reshape(D_MODEL, D_FF), w_down_b.reshape(D_FF, D_MODEL),
              g0, b0, row(ln1_g[0]), row(ln1_b[0]), row(ln2_g[0]), row(ln2_b[0]), tm=512, tf=1024)
    return h2.reshape(nb, seq, d)
```

```python
import functools
import math

import jax
import jax.numpy as jnp
import numpy as np
from jax import lax
from jax.experimental import pallas as pl
from jax.experimental.pallas import tpu as pltpu

F32 = jnp.float32
BF16 = jnp.bfloat16

D_MODEL = 2048
N_META = 16
CHUNK = 128
PAD = CHUNK - N_META
D_SSM = 2048
D_ATT = 2048
SSM_HEAD_DIM = 64
SSM_HEADS = 32
SSM_GROUPS = 8
SSM_HPG = 4
GROUP_W = SSM_HPG * SSM_HEAD_DIM
D_STATE = 128
CONV_K = 4
D_CONV = D_SSM + 2 * SSM_GROUPS * D_STATE
ATT_V_DIM = 128
ATT_HEADS = 16
ATT_QK_DIM = 64
D_FF = 4 * D_MODEL
DEPTH = 1
ALPHA = (2 * DEPTH) ** 0.25
LN_EPS = 1e-5
RMS_EPS = 1e-5
NEG_INF = -1e30
LOG2E = math.log2(math.e)
LAMBDA_INIT = 0.8 - 0.6 * math.exp(-0.3 * 0)

COL_Z = 0
COL_X = D_SSM
COL_B = COL_X + D_SSM
COL_C = COL_B + SSM_GROUPS * D_STATE
W_IN_DT = D_SSM + D_CONV
W_IN_Q = W_IN_DT + SSM_HEADS
N_QKV = 3 * D_ATT
COL_Q = 0
COL_K = D_ATT
COL_V = 2 * D_ATT
DT_ROWS = 8

V7X_VMEM_BYTES = 64 * 1024 * 1024
VMEM_LIMIT = 56 * 1024 * 1024

ATT_HEADS_PER_STEP = 2
ATT_TQ = 512
MLP_ACC_SLAB = 512


def _layer_norm_rows(x, g, b):
    mu = jnp.mean(x, axis=-1, keepdims=True)
    xc = x - mu
    var = jnp.mean(xc * xc, axis=-1, keepdims=True)
    return xc * lax.rsqrt(var + LN_EPS) * g + b


def _silu(x):
    hx = 0.5 * x
    return hx + hx * jnp.tanh(hx)


def _softplus(x):
    return jnp.maximum(x, 0.0) + jnp.log1p(jnp.exp(-jnp.abs(x)))


CONTRACT_LAST = (((1,), (1,)), ((), ()))


def _ln0_kernel(x_ref, xm_ref, g_ref, b_ref, wdt_ref, hn_ref, dt_ref, hm_ref, dtm_ref, *, ln_rows):
    def body(r, carry):
        r0 = pl.multiple_of(r * ln_rows, ln_rows)
        hn = _layer_norm_rows(x_ref[pl.ds(r0, ln_rows), :], g_ref[...], b_ref[...])
        hn_ref[pl.ds(r0, ln_rows), :] = hn.astype(BF16)
        return carry

    lax.fori_loop(0, x_ref.shape[0] // ln_rows, body, 0)
    dt_ref[...] = lax.dot_general(wdt_ref[...], hn_ref[...], CONTRACT_LAST, preferred_element_type=F32)

    @pl.when(pl.program_id(0) == 0)
    def _():
        hm_ref[...] = _layer_norm_rows(xm_ref[...], g_ref[...], b_ref[...]).astype(BF16)
        dtm_ref[...] = lax.dot_general(wdt_ref[...], hm_ref[...], CONTRACT_LAST, preferred_element_type=F32)


def _ln0(x2d, meta_pad, g, b, wdt, *, tm):
    m = x2d.shape[0]
    ndt = wdt.shape[0]
    const = lambda i: (0, 0)
    return pl.pallas_call(
        functools.partial(_ln0_kernel, ln_rows=128),
        grid=(m // tm,),
        in_specs=[
            pl.BlockSpec((tm, D_MODEL), lambda i: (i, 0)),
            pl.BlockSpec((CHUNK, D_MODEL), const),
            pl.BlockSpec((1, D_MODEL), const),
            pl.BlockSpec((1, D_MODEL), const),
            pl.BlockSpec((ndt, D_MODEL), const),
        ],
        out_specs=[
            pl.BlockSpec((tm, D_MODEL), lambda i: (i, 0)),
            pl.BlockSpec((ndt, tm), lambda i: (0, i)),
            pl.BlockSpec((CHUNK, D_MODEL), const),
            pl.BlockSpec((ndt, CHUNK), const),
        ],
        out_shape=[
            jax.ShapeDtypeStruct((m, D_MODEL), BF16),
            jax.ShapeDtypeStruct((ndt, m), F32),
            jax.ShapeDtypeStruct((CHUNK, D_MODEL), BF16),
            jax.ShapeDtypeStruct((ndt, CHUNK), F32),
        ],
        compiler_params=pltpu.CompilerParams(dimension_semantics=("arbitrary",), vmem_limit_bytes=VMEM_LIMIT),
        name="ln0",
    )(x2d, meta_pad, g, b, wdt)


def _inproj_kernel(hn_ref, hm_ref, wt_ref, o_ref, om_ref, wb_ref):
    j = pl.program_id(0)
    tn = wb_ref.shape[0]
    new_tile = pl.program_id(1) == 0
    unused = j < COL_K // tn

    @pl.when(new_tile)
    def _():
        wb_ref[...] = wt_ref[...].reshape(wb_ref.shape).astype(BF16)

    @pl.when(new_tile & jnp.logical_not(unused))
    def _():
        om_ref[...] = lax.dot_general(hm_ref[...], wb_ref[...], CONTRACT_LAST,
                                      preferred_element_type=F32).astype(om_ref.dtype)

    @pl.when(new_tile & unused)
    def _():
        om_ref[...] = jnp.zeros(om_ref.shape, om_ref.dtype)

    o_ref[...] = lax.dot_general(hn_ref[...], wb_ref[...], CONTRACT_LAST,
                                 preferred_element_type=F32).astype(o_ref.dtype)


def _inproj(hn, hm, wt, *, tm, tn):
    m = hn.shape[0]
    n = N_QKV
    assert n % tn == 0 and tn % SSM_HEADS == 0 and wt.shape[1] == SSM_HEADS and W_IN_Q % SSM_HEADS == 0
    groups = tn // SSM_HEADS
    group_start = lambda j: W_IN_Q // SSM_HEADS + j * groups
    return pl.pallas_call(
        _inproj_kernel,
        grid=(n // tn, m // tm),
        in_specs=[
            pl.BlockSpec((tm, D_MODEL), lambda j, i: (i, 0)),
            pl.BlockSpec((CHUNK, D_MODEL), lambda j, i: (0, 0)),
            pl.BlockSpec((pl.Element(groups), pl.Element(SSM_HEADS), pl.Element(D_MODEL)),
                         lambda j, i: (group_start(j), 0, 0)),
        ],
        out_specs=[
            pl.BlockSpec((tm, tn), lambda j, i: (i, j)),
            pl.BlockSpec((CHUNK, tn), lambda j, i: (0, j)),
        ],
        out_shape=[
            jax.ShapeDtypeStruct((m, n), BF16),
            jax.ShapeDtypeStruct((CHUNK, n), BF16),
        ],
        scratch_shapes=[pltpu.VMEM((tn, D_MODEL), BF16)],
        compiler_params=pltpu.CompilerParams(
            dimension_semantics=("arbitrary", "arbitrary"), vmem_limit_bytes=VMEM_LIMIT),
        name="inproj",
    )(hn, hm, wt)


U_LEAD = 16
U_SEQ0 = U_LEAD + CHUNK


def _conv_silu(u_ref, r0, w_ref, b_ref):
    win = u_ref[pl.ds(r0 - 8, CHUNK + 8), :]
    w = w_ref[...]
    acc = b_ref[...] + w[3:4, :] * win[8:CHUNK + 8, :]
    acc = acc + w[2:3, :] * win[7:CHUNK + 7, :]
    acc = acc + w[1:2, :] * win[6:CHUNK + 6, :]
    acc = acc + w[0:1, :] * win[5:CHUNK + 5, :]
    return _silu(acc)


SSD_PROJ_ROWS = 512


def _ssd_kernel(hn_ref, hmeta_ref, wz_ref, wx_ref, wb_ref, wc_ref, dt_ref, dtm_ref,
                cwx_ref, cwb_ref, cwc_ref, cbx_ref, cbb_ref, cbc_ref,
                dtb_ref, alog_ref, dskip_ref, nw_ref,
                y_ref, w_ref, z_ref, ux_ref, ub_ref, uc_ref, st_ref, m1_ref, tril_ref, m2_ref, hm_ref):
    seq = hn_ref.shape[0]

    li = lax.broadcasted_iota(jnp.int32, (CHUNK, CHUNK), 0)
    ki = lax.broadcasted_iota(jnp.int32, (CHUNK, CHUNK), 1)
    lower = jnp.where(ki <= li, 1.0, 0.0)
    tril_ref[...] = lower
    m1_ref[...] = lower.astype(BF16)
    k2 = lax.broadcasted_iota(jnp.int32, (2 * CHUNK, 2 * CHUNK), 0) & (CHUNK - 1)
    s2 = lax.broadcasted_iota(jnp.int32, (2 * CHUNK, 2 * CHUNK), 1)
    m2_ref[...] = jnp.where((s2 >= CHUNK) | (k2 > s2), 1.0, 0.0).astype(BF16)
    head_of = lax.broadcasted_iota(jnp.int32, (CHUNK, GROUP_W), 1) >> 6
    for r in range(SSM_HPG):
        hm_ref[r] = jnp.where(head_of == r, 1.0, 0.0).astype(BF16)

    nz, nx, nbc = GROUP_W, GROUP_W, D_STATE
    w_ref[0:nz, :] = wz_ref[...].reshape(nz, D_MODEL).astype(BF16)
    w_ref[nz:nz + nx, :] = wx_ref[...].reshape(nx, D_MODEL).astype(BF16)
    w_ref[nz + nx:nz + nx + nbc, :] = wb_ref[...].reshape(nbc, D_MODEL).astype(BF16)
    w_ref[nz + nx + nbc:, :] = wc_ref[...].reshape(nbc, D_MODEL).astype(BF16)

    for u_ref in (ux_ref, ub_ref, uc_ref):
        u_ref[0:U_LEAD, :] = jnp.zeros((U_LEAD, u_ref.shape[1]), F32)
    st_ref[...] = jnp.zeros(st_ref.shape, F32)
    meta_valid = lax.broadcasted_iota(jnp.int32, (CHUNK, 1), 0) >= PAD

    def project(lhs, u_row0, z_row0, meta_rows):
        res = lax.dot_general(lhs, w_ref[...], CONTRACT_LAST, preferred_element_type=F32)
        if meta_rows:
            head = jnp.where(meta_valid, res[:meta_rows], 0.0)
            res_u = jnp.concatenate([head, res[meta_rows:]], axis=0)
        else:
            res_u = res
        n = res.shape[0]
        ux_ref[u_row0:u_row0 + n, :] = res_u[:, nz:nz + nx]
        ub_ref[u_row0:u_row0 + n, :] = res_u[:, nz + nx:nz + nx + nbc]
        uc_ref[u_row0:u_row0 + n, :] = res_u[:, nz + nx + nbc:]
        z_ref[z_row0:z_row0 + n - meta_rows, :] = res[meta_rows:, 0:nz]

    def project_block(blk):
        r0 = blk * SSD_PROJ_ROWS
        if blk == 0:
            lhs = jnp.concatenate([hmeta_ref[...], hn_ref[0:SSD_PROJ_ROWS, :]], axis=0)
            project(lhs, U_LEAD, 0, CHUNK)
        else:
            project(hn_ref[r0:r0 + SSD_PROJ_ROWS, :], U_SEQ0 + r0, r0, 0)

    dt_bias = dtb_ref[0]
    a_neg = -jnp.exp(alog_ref[0])

    def chunk_step(r0, dt8, z_rows, out_row0):
        xc = _conv_silu(ux_ref, r0, cwx_ref, cbx_ref)
        bc = _conv_silu(ub_ref, r0, cwb_ref, cbb_ref)
        a8 = dt8 * a_neg
        xs_bf = xc.astype(BF16)
        bt = bc.T
        st_old = st_ref[...]
        want_y = z_rows is not None
        if want_y:
            cc = _conv_silu(uc_ref, r0, cwc_ref, cbc_ref)
            cb = lax.dot_general(cc.astype(BF16), bc.astype(BF16), CONTRACT_LAST,
                                 preferred_element_type=F32)
            cb_low = cb * tril_ref[...]
            st_bf = st_old.astype(BF16)
        lhs_y, rhs_y, lhs_s, rhs_s, tot = [], [], [], [], []
        for r in range(SSM_HPG):
            a_r = a8[r:r + 1, :]
            dt_r = dt8[r:r + 1, :]
            a_hi = a_r.astype(BF16)
            a_lo = (a_r - a_hi.astype(F32)).astype(BF16)
            m1 = m1_ref[...]
            dd = jnp.dot(jnp.concatenate([m1 * a_hi, m1 * a_lo], axis=1), m2_ref[...],
                         preferred_element_type=F32)
            dseg = dd[:, :CHUNK]
            acsb = dd[:, CHUNK:]
            lmat = jnp.exp(dseg)
            xs_r = xs_bf * hm_ref[r]
            if want_y:
                lhs_y.append((cb_low * lmat * dt_r).astype(BF16))
                lhs_y.append((cc * jnp.exp(acsb)).astype(BF16))
                rhs_y.append(xs_r)
                rhs_y.append(st_bf * hm_ref[r])
            f1 = lmat[CHUNK - 1:CHUNK, :] * dt_r
            lhs_s.append((bt * f1).astype(BF16))
            rhs_s.append(xs_r)
            tot.append(acsb[CHUNK - 1:CHUNK, :])
        s_new = jnp.dot(jnp.concatenate(lhs_s, axis=1), jnp.concatenate(rhs_s, axis=0),
                        preferred_element_type=F32)
        half = lax.broadcasted_iota(jnp.int32, (1, CHUNK), 1) < SSM_HEAD_DIM
        decay = jnp.exp(jnp.concatenate([jnp.where(half, tot[0], tot[1]),
                                         jnp.where(half, tot[2], tot[3])], axis=1))
        st_ref[...] = st_old * decay + s_new
        if want_y:
            y = jnp.dot(jnp.concatenate(lhs_y, axis=1), jnp.concatenate(rhs_y, axis=0),
                        preferred_element_type=F32)
            y = y + xc * dskip_ref[...]
            gated = y * _silu(z_rows.astype(F32))
            ms = jnp.mean(gated * gated, axis=-1, keepdims=True)
            y_ref[0, out_row0:out_row0 + CHUNK, :] = (gated * lax.rsqrt(ms + RMS_EPS) * nw_ref[...]).astype(y_ref.dtype)

    def meta_chunk():
        lane_pos = lax.broadcasted_iota(jnp.int32, (1, CHUNK), 1)
        dt_meta = jnp.where(lane_pos >= PAD, _softplus(dtm_ref[0] + dt_bias), 0.0)
        chunk_step(U_LEAD, dt_meta, None, None)

    def seq_chunk(s):
        row0 = s * CHUNK
        dt8 = _softplus(dt_ref[0, :, row0:row0 + CHUNK] + dt_bias)
        chunk_step(U_SEQ0 + row0, dt8, z_ref[row0:row0 + CHUNK, :], row0)

    per_blk = SSD_PROJ_ROWS // CHUNK
    n_blk = seq // SSD_PROJ_ROWS
    project_block(0)
    for blk in range(n_blk):
        if blk + 1 < n_blk:
            project_block(blk + 1)
        if blk == 0:
            meta_chunk()
        for s in range(blk * per_blk, (blk + 1) * per_blk):
            seq_chunk(s)


def _ssd(hn, hm, wt3, dt3, dt3_meta, conv_w, conv_b2, dtb3, alog3, dskip_row, nw_row, *, nb):
    m = hn.shape[0]
    seq = m // nb
    u_rows = U_SEQ0 + seq
    grp = lambda col0, width: (lambda b, g: ((col0 + g * width) // SSM_HEADS, 0, 0))
    w_blk = lambda col0, width: pl.BlockSpec(
        (pl.Element(width // SSM_HEADS), pl.Element(SSM_HEADS), pl.Element(D_MODEL)), grp(col0, width))
    return pl.pallas_call(
        _ssd_kernel,
        grid=(nb, SSM_GROUPS),
        in_specs=[
            pl.BlockSpec((seq, D_MODEL), lambda b, g: (b, 0)),
            pl.BlockSpec((CHUNK, D_MODEL), lambda b, g: (0, 0)),
            w_blk(COL_Z, GROUP_W), w_blk(COL_X, GROUP_W), w_blk(COL_B, D_STATE), w_blk(COL_C, D_STATE),
            pl.BlockSpec((1, DT_ROWS, seq), lambda b, g: (g, 0, b)),
            pl.BlockSpec((1, DT_ROWS, CHUNK), lambda b, g: (g, 0, 0)),
            pl.BlockSpec((CONV_K, GROUP_W), lambda b, g: (0, g)),
            pl.BlockSpec((CONV_K, D_STATE), lambda b, g: (0, D_SSM // D_STATE + g)),
            pl.BlockSpec((CONV_K, D_STATE), lambda b, g: (0, D_SSM // D_STATE + SSM_GROUPS + g)),
            pl.BlockSpec((1, GROUP_W), lambda b, g: (0, g)),
            pl.BlockSpec((1, D_STATE), lambda b, g: (0, D_SSM // D_STATE + g)),
            pl.BlockSpec((1, D_STATE), lambda b, g: (0, D_SSM // D_STATE + SSM_GROUPS + g)),
            pl.BlockSpec((1, DT_ROWS, 1), lambda b, g: (g, 0, 0)),
            pl.BlockSpec((1, DT_ROWS, 1), lambda b, g: (g, 0, 0)),
            pl.BlockSpec((1, GROUP_W), lambda b, g: (0, g)),
            pl.BlockSpec((1, GROUP_W), lambda b, g: (0, g)),
        ],
        out_specs=pl.BlockSpec((1, seq, GROUP_W), lambda b, g: (b, 0, g)),
        out_shape=jax.ShapeDtypeStruct((nb, seq, D_SSM), BF16),
        scratch_shapes=[
            pltpu.VMEM((2 * GROUP_W + 2 * D_STATE, D_MODEL), BF16),
            pltpu.VMEM((seq, GROUP_W), F32),
            pltpu.VMEM((u_rows, GROUP_W), F32),
            pltpu.VMEM((u_rows, D_STATE), F32),
            pltpu.VMEM((u_rows, D_STATE), F32),
            pltpu.VMEM((D_STATE, GROUP_W), F32),
            pltpu.VMEM((CHUNK, CHUNK), BF16),
            pltpu.VMEM((CHUNK, CHUNK), F32),
            pltpu.VMEM((2 * CHUNK, 2 * CHUNK), BF16),
            pltpu.VMEM((SSM_HPG, CHUNK, GROUP_W), BF16),
        ],
        compiler_params=pltpu.CompilerParams(
            dimension_semantics=("arbitrary", "arbitrary"), vmem_limit_bytes=VMEM_LIMIT),
        name="ssd",
    )(hn, hm, wt3, wt3, wt3, wt3, dt3, dt3_meta,
      conv_w, conv_w, conv_w, conv_b2, conv_b2, conv_b2, dtb3, alog3, dskip_row, nw_row)


def _attn_kernel(slope_ref, q_ref, k_ref, v_ref, km_ref, vm_ref, lam_ref, nw_ref, wu_ref, wd_ref,
                 o_ref, wub_ref, wdb_ref, kaug_ref, vt_ref, s_ref, p_ref):
    wub_ref[...] = wu_ref[...].astype(BF16)
    wdb_ref[...] = wd_ref[...].astype(BF16)

    hp = pl.program_id(1)
    seq = q_ref.shape[1]
    tq = ATT_TQ
    slopes = [slope_ref[ATT_HEADS_PER_STEP * hp + hh] for hh in range(ATT_HEADS_PER_STEP)]
    head_lanes = [slice(ATT_V_DIM * hh, ATT_V_DIM * (hh + 1)) for hh in range(ATT_HEADS_PER_STEP)]

    sub8 = lax.broadcasted_iota(jnp.int32, (8, CHUNK), 0)
    lane8 = lax.broadcasted_iota(jnp.int32, (8, CHUNK), 1)

    def stage(hh, r0, k, v, valid_from):
        pos = lane8 + r0
        bias = (slopes[hh] * LOG2E) * pos.astype(F32)
        if valid_from:
            bias = jnp.where(pos >= valid_from, bias, NEG_INF)
        b1 = bias.astype(BF16).astype(F32)
        b2 = (bias - b1).astype(BF16).astype(F32)
        b3 = bias - b1 - b2
        rows8 = jnp.where(sub8 == 0, b1, jnp.where(sub8 == 1, b2, jnp.where(sub8 == 2, b3, 0.0)))
        cols = jnp.concatenate([rows8, jnp.zeros((CHUNK - 8, CHUNK), F32)], axis=0).T
        kaug_ref[hh, r0:r0 + CHUNK, 0:128] = (k.astype(F32) * LOG2E).astype(BF16)
        kaug_ref[hh, r0:r0 + CHUNK, 128:256] = cols.astype(BF16)
        vt_ref[hh, 0:ATT_V_DIM, r0:r0 + CHUNK] = v.T

    for hh, hl in enumerate(head_lanes):
        vt_ref[hh, ATT_V_DIM:, :] = jnp.ones((vt_ref.shape[1] - ATT_V_DIM, vt_ref.shape[2]), BF16)
        stage(hh, 0, km_ref[:, hl], vm_ref[:, hl], PAD)
        for j in range(seq // CHUNK):
            rows = slice(CHUNK * j, CHUNK * (j + 1))
            stage(hh, CHUNK * (j + 1), k_ref[0, rows, hl], v_ref[0, rows, hl], 0)

    lam4 = lam_ref[...]
    lam = (jnp.exp(jnp.sum(lam4[0:1] * lam4[1:2], axis=-1, keepdims=True))
           - jnp.exp(jnp.sum(lam4[2:3] * lam4[3:4], axis=-1, keepdims=True)) + LAMBDA_INIT)

    lane_q = lax.broadcasted_iota(jnp.int32, (tq, 128), 1)
    ones_blk = jnp.where(lane_q < 3, 1.0, 0.0).astype(BF16)
    map_lanes = (lane_q < ATT_QK_DIM, lane_q >= ATT_QK_DIM)
    hq = tq // 2
    tri = (lax.broadcasted_iota(jnp.int32, (hq, hq), 0)
           <= lax.broadcasted_iota(jnp.int32, (hq, hq), 1))

    def scores(hh, rows, qa):
        return lax.dot_general(kaug_ref[hh, rows, :], qa, CONTRACT_LAST, preferred_element_type=F32)

    class Item:
        def __init__(self, t, c, hh):
            self.t, self.c, self.hh = t, c, hh
            self.slot = 2 * hh + c
            d0 = CHUNK + tq * t
            self.kend = d0 + tq
            self.rows_a, self.rows_b = slice(d0, d0 + hq), slice(d0 + hq, self.kend)
            self.blocks = [slice(0, CHUNK)] + [slice(CHUNK + tq * j, CHUNK + tq * (j + 1)) for j in range(t)]
            self.m = None
            self.qaug = None

        def score_steps(self):
            c, hh, slot = self.c, self.hh, self.slot

            def first():
                q = q_ref[0, tq * self.t:tq * (self.t + 1), head_lanes[hh]] * jnp.asarray(ATT_QK_DIM ** -0.5, BF16)
                self.qaug = jnp.concatenate([jnp.where(map_lanes[c], q, jnp.zeros_like(q)), ones_blk], axis=1)

            def full(rows):
                def step():
                    sv = scores(hh, rows, self.qaug)
                    s_ref[slot,rows, :] = sv
                    bm = jnp.max(sv, axis=0, keepdims=True)
                    self.m = bm if self.m is None else jnp.maximum(self.m, bm)
                return step

            def diag_a():
                sv = scores(hh, self.rows_a, self.qaug)
                sv = jnp.concatenate([jnp.where(tri, sv[:, :hq], NEG_INF), sv[:, hq:]], axis=1)
                s_ref[slot,self.rows_a, :] = sv
                self.m = jnp.maximum(self.m, jnp.max(sv, axis=0, keepdims=True))

            def diag_b():
                sv = jnp.where(tri, scores(hh, self.rows_b, self.qaug[hq:, :]), NEG_INF)
                s_ref[slot,self.rows_b, hq:] = sv
                m = self.m
                self.m = jnp.concatenate(
                    [m[:, :hq], jnp.maximum(m[:, hq:], jnp.max(sv, axis=0, keepdims=True))], axis=1)

            return [first] + [full(r) for r in self.blocks] + [diag_a, diag_b]

        def prob_steps(self):
            slot = self.slot

            def full(rows):
                def step():
                    p_ref[slot,rows, :] = jnp.exp2(s_ref[slot,rows, :] - self.m).astype(BF16)
                return step

            def diag_b():
                p_ref[slot,self.rows_b, :hq] = jnp.zeros((hq, hq), BF16)
                p_ref[slot,self.rows_b, hq:] = jnp.exp2(s_ref[slot,self.rows_b, hq:] - self.m[:, hq:]).astype(BF16)

            return [full(r) for r in self.blocks + [self.rows_a]] + [diag_b]

        def value_steps(self):
            self.acc = None

            def part(rows):
                def step():
                    d = jnp.dot(vt_ref[self.hh, :, rows], p_ref[self.slot, rows, :], preferred_element_type=F32)
                    self.acc = d if self.acc is None else self.acc + d
                return step

            return [part(r) for r in self.blocks + [slice(self.rows_a.start, self.kend)]]

        def value_out(self):
            return self.acc[:ATT_V_DIM] / self.acc[ATT_V_DIM:ATT_V_DIM + 1]

    items = [Item(t, c, hh) for t in range(seq // tq) for c in range(2) for hh in range(ATT_HEADS_PER_STEP)]
    n_items = len(items)
    outs = {}

    def finish(i):
        item = items[i]
        outs[(item.t, item.hh, item.c)] = item.value_out()
        if item.c == 1:
            o = outs.pop((item.t, item.hh, 0)) - lam * outs.pop((item.t, item.hh, 1))
            ms = jnp.mean(o * o, axis=0, keepdims=True)
            o = o * lax.rsqrt(ms + RMS_EPS) * nw_ref[...] * (1.0 - LAMBDA_INIT)
            o_ref[0, tq * item.t:tq * (item.t + 1), head_lanes[item.hh]] = o.T.astype(o_ref.dtype)

    for stage in range(-2, n_items):
        lanes = []
        if 0 <= stage + 2 < n_items:
            lanes.append(items[stage + 2].score_steps())
        if 0 <= stage + 1 < n_items:
            lanes.append(items[stage + 1].prob_steps())
        if 0 <= stage:
            lanes.append(items[stage].value_steps() + [functools.partial(finish, stage)])
        for k in range(max(len(steps) for steps in lanes)):
            for steps in lanes:
                if k < len(steps):
                    steps[k]()


def _attention(slopes, proj3, proj_meta, lam4, nw_col, w_up, w_down):
    nb, seq, _ = proj3.shape
    steps = nb * ATT_HEADS // ATT_HEADS_PER_STEP
    pair_w = ATT_V_DIM * ATT_HEADS_PER_STEP
    wu3 = w_up.reshape(steps, D_MODEL // steps, D_FF)
    wd3 = w_down.reshape(steps, D_FF // steps, D_MODEL)
    w_blk = lambda a: pl.BlockSpec((1,) + a.shape[1:], lambda b, h: (b * (ATT_HEADS // ATT_HEADS_PER_STEP) + h, 0, 0))
    cq, ck, cv = COL_Q // pair_w, COL_K // pair_w, COL_V // pair_w
    kv_rows = CHUNK + seq
    seq_blk = lambda c0: pl.BlockSpec((1, seq, pair_w), lambda b, h: (b, 0, c0 + h))
    meta_blk = lambda c0: pl.BlockSpec((CHUNK, pair_w), lambda b, h: (0, c0 + h))
    return pl.pallas_call(
        _attn_kernel,
        grid=(nb, ATT_HEADS // ATT_HEADS_PER_STEP),
        in_specs=[
            pl.BlockSpec(memory_space=pltpu.SMEM),
            seq_blk(cq), seq_blk(ck), seq_blk(cv), meta_blk(ck), meta_blk(cv),
            pl.BlockSpec((4, ATT_QK_DIM), lambda b, h: (0, 0)),
            pl.BlockSpec((ATT_V_DIM, 1), lambda b, h: (0, 0)),
            w_blk(wu3), w_blk(wd3),
        ],
        out_specs=[pl.BlockSpec((1, seq, pair_w), lambda b, h: (b, 0, h)), w_blk(wu3), w_blk(wd3)],
        out_shape=[jax.ShapeDtypeStruct((nb, seq, D_ATT), BF16),
                   jax.ShapeDtypeStruct(wu3.shape, BF16), jax.ShapeDtypeStruct(wd3.shape, BF16)],
        scratch_shapes=[
            pltpu.VMEM((ATT_HEADS_PER_STEP, kv_rows, 2 * ATT_V_DIM), BF16),
            pltpu.VMEM((ATT_HEADS_PER_STEP, ATT_V_DIM + 16, kv_rows), BF16),
            pltpu.VMEM((2 * ATT_HEADS_PER_STEP, kv_rows, ATT_TQ), F32),
            pltpu.VMEM((2 * ATT_HEADS_PER_STEP, kv_rows, ATT_TQ), BF16),
        ],
        compiler_params=pltpu.CompilerParams(
            dimension_semantics=("arbitrary", "arbitrary"), vmem_limit_bytes=VMEM_LIMIT),
        name="diffattn",
    )(slopes, proj3, proj3, proj3, proj_meta, proj_meta, lam4, nw_col, wu3, wd3)


def _outproj_kernel(y_ref, o_ref, wy_ref, wo_ref, mix_ref, wyb_ref, wob_ref):
    @pl.when(pl.program_id(1) == 0)
    def _():
        wyb_ref[...] = wy_ref[...].astype(BF16)
        wob_ref[...] = wo_ref[...].astype(BF16)

    mix_ref[...] = (jnp.dot(y_ref[...], wyb_ref[...], preferred_element_type=F32)
                    + jnp.dot(o_ref[...], wob_ref[...], preferred_element_type=F32))


def _outproj(y2d, o2d, w_out, *, tm, tn):
    m = y2d.shape[0]
    return pl.pallas_call(
        _outproj_kernel,
        grid=(D_MODEL // tn, m // tm),
        in_specs=[
            pl.BlockSpec((tm, D_SSM), lambda j, i: (i, 0)),
            pl.BlockSpec((tm, D_ATT), lambda j, i: (i, 0)),
            pl.BlockSpec((D_SSM, tn), lambda j, i: (0, j)),
            pl.BlockSpec((D_ATT, tn), lambda j, i: (1, j)),
        ],
        out_specs=pl.BlockSpec((tm, tn), lambda j, i: (i, j)),
        out_shape=jax.ShapeDtypeStruct((m, D_MODEL), F32),
        scratch_shapes=[pltpu.VMEM((D_SSM, tn), BF16), pltpu.VMEM((D_ATT, tn), BF16)],
        compiler_params=pltpu.CompilerParams(
            dimension_semantics=("arbitrary", "arbitrary"), vmem_limit_bytes=VMEM_LIMIT),
        name="outproj",
    )(y2d, o2d, w_out, w_out)


def _mlp_kernel(x_ref, mix_ref, wu_ref, wd_ref, g0_ref, b0_ref, g1_ref, b1_ref, g2_ref, b2_ref,
                o_ref, h1_ref, hb_ref, *, ln_rows):
    f = pl.program_id(1)
    n_ln = x_ref.shape[0] // ln_rows

    @pl.when(f == 0)
    def _():
        def body(r, carry):
            rows = pl.ds(pl.multiple_of(r * ln_rows, ln_rows), ln_rows)
            h0 = _layer_norm_rows(x_ref[rows, :], g0_ref[...], b0_ref[...])
            h1 = _layer_norm_rows(ALPHA * h0 + mix_ref[rows, :], g1_ref[...], b1_ref[...])
            h1_ref[rows, :] = h1
            hb_ref[rows, :] = h1.astype(BF16)
            return carry

        lax.fori_loop(0, n_ln, body, 0)
        o_ref[...] = jnp.zeros(o_ref.shape, F32)

    u = jnp.dot(hb_ref[...], wu_ref[...], preferred_element_type=F32)
    u = jnp.square(jnp.maximum(u, 0.0)).astype(BF16)
    for c0 in range(0, D_MODEL, MLP_ACC_SLAB):
        cols = slice(c0, c0 + MLP_ACC_SLAB)
        o_ref[:, cols] += jnp.dot(u, wd_ref[:, cols], preferred_element_type=F32)

    @pl.when(f == pl.num_programs(1) - 1)
    def _():
        def body(r, carry):
            rows = pl.ds(pl.multiple_of(r * ln_rows, ln_rows), ln_rows)
            o_ref[rows, :] = _layer_norm_rows(ALPHA * h1_ref[rows, :] + o_ref[rows, :], g2_ref[...], b2_ref[...])
            return carry

        lax.fori_loop(0, n_ln, body, 0)


def _mlp(x2d, mix, w_up, w_down, g0, b0, g1, b1, g2, b2, *, tm, tf):
    m = x2d.shape[0]
    row = pl.BlockSpec((1, D_MODEL), lambda i, f: (0, 0))
    return pl.pallas_call(
        functools.partial(_mlp_kernel, ln_rows=128),
        grid=(m // tm, D_FF // tf),
        in_specs=[
            pl.BlockSpec((tm, D_MODEL), lambda i, f: (i, 0)),
            pl.BlockSpec((tm, D_MODEL), lambda i, f: (i, 0)),
            pl.BlockSpec((D_MODEL, tf), lambda i, f: (0, f)),
            pl.BlockSpec((tf, D_MODEL), lambda i, f: (f, 0)),
            row, row, row, row, row, row,
        ],
        out_specs=pl.BlockSpec((tm, D_MODEL), lambda i, f: (i, 0)),
        out_shape=jax.ShapeDtypeStruct((m, D_MODEL), F32),
        scratch_shapes=[pltpu.VMEM((tm, D_MODEL), F32), pltpu.VMEM((tm, D_MODEL), BF16)],
        compiler_params=pltpu.CompilerParams(
            dimension_semantics=("arbitrary", "arbitrary"), vmem_limit_bytes=VMEM_LIMIT),
        name="mlp",
    )(x2d, mix, w_up, w_down, g0, b0, g1, b1, g2, b2)


def kernel(x, meta_tokens, ln0_g, ln0_b, w_in, conv_w, conv_b, dt_bias, a_log, d_skip, ssd_norm_w,
           lambda_q1, lambda_k1, lambda_q2, lambda_k2, attn_norm_w, w_out, ln1_g, ln1_b, w_up, w_down,
           ln2_g, ln2_b):
    nb, seq, d = x.shape
    assert (d, w_in.shape[0]) == (D_MODEL, DEPTH) and seq % ATT_TQ == 0
    m = nb * seq
    x2d = x.reshape(m, d)
    row = lambda v: v.reshape(1, -1).astype(F32)

    w_in_t = w_in[0].astype(F32).T
    wdt = w_in_t[W_IN_DT:W_IN_DT + SSM_HEADS].reshape(SSM_GROUPS, SSM_HPG, d)
    wdt = jnp.pad(wdt, ((0, 0), (0, DT_ROWS - SSM_HPG), (0, 0))).reshape(SSM_GROUPS * DT_ROWS, d).astype(BF16)
    pad_heads = lambda v: jnp.pad(v.reshape(SSM_GROUPS, SSM_HPG).astype(F32),
                                  ((0, 0), (0, DT_ROWS - SSM_HPG)))[..., None]
    dtb3, alog3 = pad_heads(dt_bias[0]), pad_heads(a_log[0])
    dskip_row = row(jnp.repeat(d_skip[0], SSM_HEAD_DIM))
    lam4 = jnp.stack([lambda_q1[0], lambda_k1[0], lambda_q2[0], lambda_k2[0]]).astype(F32)
    slopes = jnp.asarray(2.0 ** (-8.0 * np.arange(1, ATT_HEADS + 1) / ATT_HEADS), dtype=F32)
    g0, b0 = row(ln0_g), row(ln0_b)
    meta_pad = jnp.pad(meta_tokens.astype(F32), ((PAD, 0), (0, 0)))

    hn, dt_t, hm, dt_t_meta = _ln0(x2d, meta_pad, g0, b0, wdt, tm=1024)
    wt3 = w_in_t.reshape(-1, SSM_HEADS, d)
    qkv, qkv_meta = _inproj(hn, hm, wt3, tm=2048, tn=768)
    qkv3 = qkv.reshape(nb, seq, N_QKV)
    dt3 = dt_t.reshape(SSM_GROUPS, DT_ROWS, m)
    dt3_meta = dt_t_meta.reshape(SSM_GROUPS, DT_ROWS, CHUNK)

    y = _ssd(hn, hm, wt3, dt3, dt3_meta, conv_w[0].astype(F32), row(conv_b[0]),
             dtb3, alog3, dskip_row, row(ssd_norm_w[0]), nb=nb)
    o, w_up_b, w_down_b = _attention(slopes, qkv3, qkv_meta, lam4,
                                     attn_norm_w[0].reshape(ATT_V_DIM, 1).astype(F32),
                                     w_up[0].astype(F32), w_down[0].astype(F32))

    mix = _outproj(y.reshape(m, D_SSM), o.reshape(m, D_ATT), w_out[0].astype(F32), tm=1024, tn=512)
    h2 = _mlp(x2d, mix, w_up_b.reshape(D_MODEL, D_FF), w_down_b.reshape(D_FF, D_MODEL),
              g0, b0, row(ln1_g[0]), row(ln1_b[0]), row(ln2_g[0]), row(ln2_b[0]), tm=512, tf=1024)
    return h2.reshape(nb, seq, d)
```
